```python
import jax, jax.numpy as jnp
from jax import lax
import numpy as np

D_MODEL = 1024
BATCH = 8
SEQ = 4096
DEPTH = 4

HEAD_DIM = 128
EPS = 1e-6
ROPE_THETA = 500000.0
ROT_DIM = HEAD_DIM // 4
N_MEM = 256
MEM_HEADS = 4
DIL_GROUPS = ((128, 1), (512, 4), (2048, 16))
A_HEADS = 4
POOL_SIZES = (2, 4, 8, 16)
POOL_CH = 128
CHUNK = 128
C_GROUPS = 8
C_CH = 128

A_WIDTH = A_HEADS * HEAD_DIM
B_WIDTH = len(POOL_SIZES) * POOL_CH
C_WIDTH = C_GROUPS * C_CH
M_WIDTH = MEM_HEADS * HEAD_DIM
EVEN_MIX = A_WIDTH + B_WIDTH + M_WIDTH
ODD_MIX = C_WIDTH + M_WIDTH
A_QK_WIDTH = 2 * len(DIL_GROUPS) * A_WIDTH
EVEN_IN = A_QK_WIDTH + A_WIDTH + B_WIDTH + M_WIDTH + EVEN_MIX
ODD_IN = 2 * C_WIDTH + M_WIDTH + ODD_MIX
N_EVEN = (DEPTH + 1) // 2
N_ODD = DEPTH // 2

kernel_name = "hybrid_dilated_pool_gmlp_trunk"


def rmsnorm(x, g):
    xf = x.astype(jnp.float32)
    y = xf * lax.rsqrt(jnp.mean(xf * xf, axis=-1, keepdims=True) + EPS)
    return (y * g.astype(jnp.float32)).astype(x.dtype)


def layernorm(x, g, b):
    xf = x.astype(jnp.float32)
    mu = jnp.mean(xf, axis=-1, keepdims=True)
    xc = xf - mu
    y = xc * lax.rsqrt(jnp.mean(xc * xc, axis=-1, keepdims=True) + EPS)
    return (y * g.astype(jnp.float32) + b.astype(jnp.float32)).astype(x.dtype)


def rope_tables(positions):
    pos = positions.astype(jnp.float32)
    inv = ROPE_THETA ** (-jnp.arange(0, ROT_DIM, 2, dtype=jnp.float32) / ROT_DIM)
    ang = pos[..., None] * inv
    return jnp.cos(ang), jnp.sin(ang)


def apply_partial_rope(x, cos, sin):
    half = ROT_DIM // 2
    xr = x[..., :ROT_DIM].astype(jnp.float32)
    x1, x2 = xr[..., :half], xr[..., half:]
    c, s = cos[:, :, None, :], sin[:, :, None, :]
    rot = jnp.concatenate([x1 * c - x2 * s, x2 * c + x1 * s], axis=-1).astype(x.dtype)
    return jnp.concatenate([rot, x[..., ROT_DIM:]], axis=-1)


def _to_strided(x, dil, blk):
    b, s, h, d = x.shape
    unit = dil * blk
    lp = -(-s // unit) * unit
    x = jnp.pad(x, ((0, 0), (0, lp - s), (0, 0), (0, 0)))
    x = x.reshape(b, lp // dil, dil, h, d).transpose(0, 2, 1, 3, 4)
    return x.reshape(b, dil, lp // unit, blk, h, d)


def _from_strided(x, seq):
    b, dil, nb, blk = x.shape[:4]
    rest = x.shape[4:]
    x = jnp.moveaxis(x.reshape((b, dil, nb * blk) + rest), 1, 2)
    return x.reshape((b, nb * blk * dil) + rest)[:, :seq]


def dilated_window_group(q, k, v, window, dil):
    span = window // dil
    blk = span
    seq = q.shape[1]
    qb, kb, vb = (_to_strided(t, dil, blk) for t in (q, k, v))
    nb = qb.shape[2]

    def with_prev(t):
        prev = jnp.concatenate([jnp.zeros_like(t[:, :, :1]), t[:, :, :-1]], axis=2)
        return jnp.concatenate([prev, t], axis=3)

    kk, vv = with_prev(kb), with_prev(vb)
    scores = jnp.einsum('brnqhd,brnkhd->brnhqk', qb, kk,
                        preferred_element_type=jnp.float32) * (HEAD_DIM ** -0.5)
    qi = jnp.arange(blk)[:, None]
    kj = jnp.arange(2 * blk)[None, :]
    off = blk + qi - kj
    band = (off >= 0) & (off <= span)
    not_before_start = (jnp.arange(nb)[:, None, None] > 0) | (kj[None] >= blk)
    valid = band[None] & not_before_start
    scores = jnp.where(valid[None, None, :, None], scores, -jnp.inf)
    m = jnp.max(scores, axis=-1)
    p = jnp.exp(scores - m[..., None])
    den = jnp.sum(p, axis=-1)
    num = jnp.einsum('brnhqk,brnkhd->brnqhd', p, vv.astype(jnp.float32))
    m = jnp.swapaxes(m, 3, 4)
    den = jnp.swapaxes(den, 3, 4)
    return _from_strided(num, seq), _from_strided(m, seq), _from_strided(den, seq)


def dilated_attention_mixer(qk, v_a, cos, sin):
    nums, ms, dens = [], [], []
    for gi, (window, dil) in enumerate(DIL_GROUPS):
        q = apply_partial_rope(qk[:, :, gi, 0], cos, sin)
        k = apply_partial_rope(qk[:, :, gi, 1], cos, sin)
        num, m, den = dilated_window_group(q, k, v_a, window, dil)
        nums.append(num); ms.append(m); dens.append(den)
    ms = jnp.stack(ms)
    wts = jnp.exp(ms - jnp.max(ms, axis=0, keepdims=True))
    num = sum(wts[g][..., None] * nums[g] for g in range(len(DIL_GROUPS)))
    den = jnp.sum(wts * jnp.stack(dens), axis=0)
    out = num / den[..., None]
    b, s = out.shape[:2]
    return out.reshape(b, s, A_WIDTH).astype(v_a.dtype)


def multiscale_pool(xp, w_pool, scale):
    b, s, _ = xp.shape
    xg = xp.reshape(b, s, len(POOL_SIZES), POOL_CH).astype(jnp.float32)
    c = jnp.cumsum(xg, axis=1)
    t = jnp.arange(s)
    outs = []
    for gi, w in enumerate(POOL_SIZES):
        cg = c[:, :, gi]
        lag = jnp.pad(cg, ((0, 0), (w, 0), (0, 0)))[:, :s]
        cnt = jnp.minimum(t + 1, w).astype(jnp.float32)[None, :, None]
        outs.append((cg - lag) / cnt - xg[:, :, gi])
    pooled = jnp.stack(outs, axis=2)
    y = jnp.einsum('bsgc,gcd->bsgd', pooled, w_pool.astype(jnp.float32))
    return (y.reshape(b, s, B_WIDTH) * scale.astype(jnp.float32)).astype(xp.dtype)


def chunked_spatial_gating(u, v, ln_g, ln_b, w_s, b_s):
    b, s, _ = u.shape
    vn = layernorm(v, ln_g, ln_b)
    vc = vn.reshape(b, s // CHUNK, CHUNK, C_GROUPS, C_CH)
    tril = jnp.tril(jnp.ones((CHUNK, CHUNK), dtype=bool))
    ws = jnp.where(tril[None], w_s, jnp.zeros_like(w_s))
    mixed = jnp.einsum('gts,bnsgc->bntgc', ws, vc) + b_s.T[None, None, :, :, None]
    return u * mixed.reshape(b, s, C_WIDTH)


def memory_attention(q_m, mem_n, w_mem_kv):
    b, s = q_m.shape[:2]
    q = q_m.reshape(b, s, MEM_HEADS, HEAD_DIM)
    mk, mv = jnp.split(mem_n @ w_mem_kv, 2, axis=-1)
    mk = mk.reshape(b, -1, MEM_HEADS, HEAD_DIM)
    mv = mv.reshape(b, -1, MEM_HEADS, HEAD_DIM)
    sc = jnp.einsum('bshd,bmhd->bhsm', q, mk,
                    preferred_element_type=jnp.float32) * (HEAD_DIM ** -0.5)
    p = jax.nn.softmax(sc, axis=-1)
    out = jnp.einsum('bhsm,bmhd->bshd', p, mv.astype(jnp.float32))
    return out.reshape(b, s, M_WIDTH).astype(q_m.dtype)


def even_layer(x, cos, sin, mem_n, g_norm, w_in, w_pool, pool_scale, w_mem_kv, w_out):
    b, s, _ = x.shape
    h = rmsnorm(x, g_norm)
    proj = h @ w_in
    cuts = np.cumsum([A_QK_WIDTH, A_WIDTH, B_WIDTH, M_WIDTH]).tolist()
    qk, v_a, x_b, q_m, z = jnp.split(proj, cuts, axis=-1)
    qk = qk.reshape(b, s, len(DIL_GROUPS), 2, A_HEADS, HEAD_DIM)
    v_a = v_a.reshape(b, s, A_HEADS, HEAD_DIM)
    a_out = dilated_attention_mixer(qk, v_a, cos, sin)
    b_out = multiscale_pool(x_b, w_pool, pool_scale)
    m_out = memory_attention(q_m, mem_n, w_mem_kv)
    y = jnp.concatenate([a_out, b_out, m_out], axis=-1) * jax.nn.silu(z)
    return x + y @ w_out


def odd_layer(x, mem_n, g_norm, w_in, ln_g, ln_b, w_s, b_s, w_mem_kv, w_out):
    h = rmsnorm(x, g_norm)
    proj = h @ w_in
    cuts = np.cumsum([C_WIDTH, C_WIDTH, M_WIDTH]).tolist()
    u, v, q_m, z = jnp.split(proj, cuts, axis=-1)
    c_out = chunked_spatial_gating(u, v, ln_g, ln_b, w_s, b_s)
    m_out = memory_attention(q_m, mem_n, w_mem_kv)
    y = jnp.concatenate([c_out, m_out], axis=-1) * jax.nn.silu(z)
    return x + y @ w_out


def setup_inputs(seed: int = 0) -> dict:
    key = jax.random.key(seed)
    ks = jax.random.split(key, 24)
    nrm = jax.random.normal
    f32 = jnp.float32
    offs = jax.random.randint(ks[2], (BATCH, 1), 0, 1024)
    positions = (jnp.arange(SEQ, dtype=jnp.int32)[None, :] + offs).astype(jnp.int32)
    return {
        "x": nrm(ks[0], (BATCH, SEQ, D_MODEL), f32),
        "mem": nrm(ks[1], (BATCH, N_MEM, D_MODEL), f32),
        "positions": positions,
        "g_mem": 1.0 + 0.02 * nrm(ks[3], (D_MODEL,), f32),
        "even_norm_g": 1.0 + 0.02 * nrm(ks[4], (N_EVEN, D_MODEL), f32),
        "even_w_in": nrm(ks[5], (N_EVEN, D_MODEL, EVEN_IN), f32) * D_MODEL ** -0.5,
        "even_w_pool": nrm(ks[6], (N_EVEN, len(POOL_SIZES), POOL_CH, POOL_CH), f32) * POOL_CH ** -0.5,
        "even_pool_scale": 1.0 + 0.02 * nrm(ks[7], (N_EVEN, B_WIDTH), f32),
        "even_w_mem_kv": nrm(ks[8], (N_EVEN, D_MODEL, 2 * M_WIDTH), f32) * D_MODEL ** -0.5,
        "even_w_out": nrm(ks[9], (N_EVEN, EVEN_MIX, D_MODEL), f32) * EVEN_MIX ** -0.5,
        "odd_norm_g": 1.0 + 0.02 * nrm(ks[10], (N_ODD, D_MODEL), f32),
        "odd_w_in": nrm(ks[11], (N_ODD, D_MODEL, ODD_IN), f32) * D_MODEL ** -0.5,
        "odd_ln_g": 1.0 + 0.02 * nrm(ks[12], (N_ODD, C_WIDTH), f32),
        "odd_ln_b": 0.02 * nrm(ks[13], (N_ODD, C_WIDTH), f32),
        "odd_w_s": nrm(ks[14], (N_ODD, C_GROUPS, CHUNK, CHUNK), f32) * CHUNK ** -0.5,
        "odd_b_s": 1.0 + 0.02 * nrm(ks[15], (N_ODD, C_GROUPS, CHUNK), f32),
        "odd_w_mem_kv": nrm(ks[16], (N_ODD, D_MODEL, 2 * M_WIDTH), f32) * D_MODEL ** -0.5,
        "odd_w_out": nrm(ks[17], (N_ODD, ODD_MIX, D_MODEL), f32) * ODD_MIX ** -0.5,
        "final_norm_g": 1.0 + 0.02 * nrm(ks[18], (D_MODEL,), f32),
    }


def reference(x, mem, positions, g_mem, even_norm_g, even_w_in, even_w_pool, even_pool_scale,
              even_w_mem_kv, even_w_out, odd_norm_g, odd_w_in, odd_ln_g, odd_ln_b, odd_w_s,
              odd_b_s, odd_w_mem_kv, odd_w_out, final_norm_g):
    cos, sin = rope_tables(positions)
    mem_n = rmsnorm(mem, g_mem)
    for layer in range(DEPTH):
        i = layer // 2
        if layer % 2 == 0:
            x = even_layer(x, cos, sin, mem_n, even_norm_g[i], even_w_in[i], even_w_pool[i],
                           even_pool_scale[i], even_w_mem_kv[i], even_w_out[i])
        else:
            x = odd_layer(x, mem_n, odd_norm_g[i], odd_w_in[i], odd_ln_g[i], odd_ln_b[i],
                          odd_w_s[i], odd_b_s[i], odd_w_mem_kv[i], odd_w_out[i])
    return rmsnorm(x, final_norm_g)
```

```python
import functools

import jax
import jax.numpy as jnp
from jax import lax
from jax.experimental import pallas as pl
from jax.experimental.pallas import tpu as pltpu

D_MODEL = 1024
HEAD_DIM = 128
EPS = 1e-6
ROPE_THETA = 500000.0
ROT_DIM = HEAD_DIM // 4
ROT_HALF = ROT_DIM // 2
N_MEM = 256
MEM_HEADS = 4
DIL_GROUPS = ((128, 1), (512, 4), (2048, 16))
A_HEADS = 4
POOL_SIZES = (2, 4, 8, 16)
POOL_CH = 128
CHUNK = 128
C_GROUPS = 8
C_CH = 128

A_WIDTH = A_HEADS * HEAD_DIM
B_WIDTH = len(POOL_SIZES) * POOL_CH
C_WIDTH = C_GROUPS * C_CH
M_WIDTH = MEM_HEADS * HEAD_DIM
EVEN_MIX = A_WIDTH + B_WIDTH + M_WIDTH
ODD_MIX = C_WIDTH + M_WIDTH
N_GROUPS = len(DIL_GROUPS)
A_QK_WIDTH = 2 * N_GROUPS * A_WIDTH
QKV_WIDTH = A_QK_WIDTH + A_WIDTH
EVEN_IN = QKV_WIDTH + B_WIDTH + M_WIDTH + EVEN_MIX
EVEN_REST = EVEN_IN - QKV_WIDTH
ODD_IN = 2 * C_WIDTH + M_WIDTH + ODD_MIX
SPAN = 128
SCALE = HEAD_DIM ** -0.5
POOL_HALO = max(POOL_SIZES)

LANES = 128
ROW_TILE = 512
ROW_BAND = 128
COL_CHUNK = 512
VMEM_LIMIT = 56 * 1024 * 1024

F32 = jnp.float32
BF16 = jnp.bfloat16


def _const_spec(shape):
    nd = len(shape)
    return pl.BlockSpec(shape, lambda *_: (0,) * nd, pipeline_mode=pl.Buffered(1))


def _params(*sem):
    return pltpu.CompilerParams(dimension_semantics=sem, vmem_limit_bytes=VMEM_LIMIT)


def _rmsnorm_bf16(x, g):
    ms = jnp.mean(x * x, axis=-1, keepdims=True)
    return ((x * lax.rsqrt(ms + EPS)) * g).astype(BF16)


def _rmsnorm_rows(x_ref, g_ref, h_ref):
    g = g_ref[...]
    for r in range(0, x_ref.shape[0], ROW_BAND):
        h_ref[r:r + ROW_BAND, :] = _rmsnorm_bf16(x_ref[r:r + ROW_BAND, :], g)


def _project(h_ref, w_ref, p_ref, col0=0):
    for c in range(0, p_ref.shape[1], COL_CHUNK):
        p_ref[:, c:c + COL_CHUNK] = _dot(h_ref[...], w_ref[:, col0 + c:col0 + c + COL_CHUNK])


def _dot(a, b):
    return jnp.dot(a, b, preferred_element_type=F32)


def _dot_nt(a, b):
    return lax.dot_general(a, b, (((1,), (1,)), ((), ())), preferred_element_type=F32)


def _silu(z):
    return z * (1.0 / (1.0 + jnp.exp(-z)))


def _rope_kernel(pos_ref, inv_ref, cos_ref, sin_ref):
    ang = pos_ref[...].astype(F32) * inv_ref[...]
    cos_ref[...] = jnp.cos(ang)
    sin_ref[...] = jnp.sin(ang)


def _rope_tables(positions):
    n = positions.size
    per_row = LANES // ROT_HALF
    inv = ROPE_THETA ** (-jnp.arange(0, ROT_DIM, 2, dtype=F32) / ROT_DIM)
    pos_c = jnp.repeat(positions.reshape(n // per_row, per_row), ROT_HALF, axis=1)
    inv_c = jnp.tile(inv, per_row).reshape(1, LANES)
    rows = n // per_row
    cos_c, sin_c = pl.pallas_call(
        _rope_kernel,
        out_shape=(jax.ShapeDtypeStruct((rows, LANES), F32),) * 2,
        name="rope_tables",
    )(pos_c, inv_c)
    cos = cos_c.reshape(n, ROT_HALF)
    sin = sin_c.reshape(n, ROT_HALF)
    ones = jnp.ones((n, LANES - ROT_DIM), F32)
    zeros = jnp.zeros((n, LANES - ROT_DIM), F32)
    zhalf = jnp.zeros((n, ROT_HALF), F32)
    c_tab = jnp.concatenate([cos, cos, ones], axis=1)
    sa_tab = jnp.concatenate([zhalf, sin, zeros], axis=1)
    sb_tab = jnp.concatenate([sin, zhalf, zeros], axis=1)
    return c_tab, sa_tab, sb_tab


def _memkv_kernel(mem_ref, g_ref, w_ref, out_ref):
    mem_n = _rmsnorm_bf16(mem_ref[0], g_ref[...])
    out_ref[0, 0] = _dot(mem_n, w_ref[0]).astype(BF16)


def _memory_kv(mem, g_mem, w_kv):
    n_layers = w_kv.shape[0]
    batch = mem.shape[0]
    return pl.pallas_call(
        _memkv_kernel,
        grid=(n_layers, batch),
        in_specs=[
            pl.BlockSpec((1, N_MEM, D_MODEL), lambda l, b: (b, 0, 0)),
            pl.BlockSpec((1, D_MODEL), lambda l, b: (0, 0)),
            pl.BlockSpec((1, D_MODEL, 2 * M_WIDTH), lambda l, b: (l, 0, 0)),
        ],
        out_specs=pl.BlockSpec((1, 1, N_MEM, 2 * M_WIDTH), lambda l, b: (l, b, 0, 0)),
        out_shape=jax.ShapeDtypeStruct((n_layers, batch, N_MEM, 2 * M_WIDTH), BF16),
        compiler_params=_params("parallel", "parallel"),
        name="memory_kv",
    )(mem, g_mem.reshape(1, D_MODEL), w_kv)


def _memory_attention(p_ref, col0, mk_ref, mv_ref):
    outs = []
    for h in range(MEM_HEADS):
        lanes = slice(h * HEAD_DIM, (h + 1) * HEAD_DIM)
        q = p_ref[:, col0 + h * HEAD_DIM:col0 + (h + 1) * HEAD_DIM].astype(BF16)
        sc = _dot_nt(q, mk_ref[:, lanes]) * SCALE
        e = jnp.exp(sc - jnp.max(sc, axis=-1, keepdims=True))
        p = e / jnp.sum(e, axis=-1, keepdims=True)
        outs.append(_dot(p.astype(BF16), mv_ref[:, lanes]))
    return outs


def _qkv_kernel(x_ref, g_ref, w_ref, c_ref, sa_ref, sb_ref, *refs):
    out_refs, h_ref = refs[:-1], refs[-1]
    _rmsnorm_rows(x_ref, g_ref, h_ref)
    c_tab, sa_tab, sb_tab = c_ref[...], sa_ref[...], sb_ref[...]
    for c in range(QKV_WIDTH // COL_CHUNK):
        acc = _dot(h_ref[...], w_ref[:, c * COL_CHUNK:(c + 1) * COL_CHUNK])
        for hh in range(COL_CHUNK // HEAD_DIM):
            lanes = slice(hh * HEAD_DIM, (hh + 1) * HEAD_DIM)
            blk = acc[:, lanes]
            if c * COL_CHUNK < A_QK_WIDTH:
                blk = (blk * c_tab
                       + pltpu.roll(blk, ROT_HALF, 1) * sa_tab
                       - pltpu.roll(blk, HEAD_DIM - ROT_HALF, 1) * sb_tab)
            out_refs[c][:, lanes] = blk.astype(BF16)


def _qkv_project(x2, g, w_qkv, tabs):
    rows = x2.shape[0]
    n_out = QKV_WIDTH // COL_CHUNK
    row_spec = lambda w: pl.BlockSpec((ROW_TILE, w), lambda i: (i, 0))
    return pl.pallas_call(
        _qkv_kernel,
        grid=(rows // ROW_TILE,),
        in_specs=[row_spec(D_MODEL), _const_spec((1, D_MODEL)), _const_spec((D_MODEL, QKV_WIDTH)),
                  row_spec(LANES), row_spec(LANES), row_spec(LANES)],
        out_specs=[row_spec(COL_CHUNK)] * n_out,
        out_shape=[jax.ShapeDtypeStruct((rows, COL_CHUNK), BF16)] * n_out,
        scratch_shapes=[pltpu.VMEM((ROW_TILE, D_MODEL), BF16)],
        compiler_params=_params("parallel"),
        name="even_qkv",
    )(x2, g.reshape(1, D_MODEL), w_qkv, *tabs)


def _attn_kernel(q_ref, k_ref, v_ref, o_ref, l_ref, kprev, vprev, *, n_res, n_blk):
    step = pl.program_id(2)

    @pl.when(step == 0)
    def _():
        kprev[...] = jnp.zeros_like(kprev)
        vprev[...] = jnp.zeros_like(vprev)

    row = lax.broadcasted_iota(jnp.int32, (SPAN, 2 * SPAN), 0)
    col = lax.broadcasted_iota(jnp.int32, (SPAN, 2 * SPAN), 1)
    in_cur = (col >= SPAN) & (col - SPAN <= row)
    in_prev = (col < SPAN) & (col >= row)
    lane = lax.broadcasted_iota(jnp.int32, (SPAN, LANES), 1)
    valid_first = in_cur | (in_prev & (step > 0))
    valid_rest = in_cur | in_prev

    for rho in range(n_res):
        for jb in range(n_blk):
            rows = slice(jb * SPAN, (jb + 1) * SPAN)
            prows = slice((jb - 1) * SPAN, jb * SPAN)
            stats = jnp.zeros((SPAN, LANES), F32)
            for h in range(A_HEADS):
                lanes = slice(rho * A_WIDTH + h * HEAD_DIM, rho * A_WIDTH + (h + 1) * HEAD_DIM)
                if jb == 0:
                    kp, vp, valid = kprev[:, lanes], vprev[:, lanes], valid_first
                else:
                    kp, vp, valid = k_ref[0, prows, lanes], v_ref[0, prows, lanes], valid_rest
                kk = jnp.concatenate([kp, k_ref[0, rows, lanes]], axis=0)
                vv = jnp.concatenate([vp, v_ref[0, rows, lanes]], axis=0)
                s = _dot_nt(q_ref[0, rows, lanes], kk) * SCALE
                s = jnp.where(valid, s, -jnp.inf)
                m = jnp.max(s, axis=-1, keepdims=True)
                p = jnp.exp(s - m)
                den = jnp.sum(p, axis=-1, keepdims=True)
                num = _dot(p.astype(BF16), vv)
                o_ref[0, rows, lanes] = (num / den).astype(BF16)
                stats = jnp.where(lane == h, m + jnp.log(den), stats)
            l_ref[0, rows, rho * LANES:(rho + 1) * LANES] = stats

    last = slice((n_blk - 1) * SPAN, n_blk * SPAN)
    kprev[...] = k_ref[0, last, :]
    vprev[...] = v_ref[0, last, :]


def _dilated_attention(q, k, v, batch, seq, dil):
    steps = seq // dil
    n_res = min(dil, 4)
    n_blk = max(1, 4 // n_res)
    view = lambda a, w: a.reshape(batch, steps, dil * w)
    spec = lambda w: pl.BlockSpec((1, n_blk * SPAN, n_res * w), lambda b, r, j: (b, j, r))
    o, lse = pl.pallas_call(
        functools.partial(_attn_kernel, n_res=n_res, n_blk=n_blk),
        grid=(batch, dil // n_res, steps // (n_blk * SPAN)),
        in_specs=[spec(A_WIDTH)] * 3,
        out_specs=[spec(A_WIDTH), spec(LANES)],
        out_shape=[jax.ShapeDtypeStruct((batch, steps, dil * A_WIDTH), BF16),
                   jax.ShapeDtypeStruct((batch, steps, dil * LANES), F32)],
        scratch_shapes=[pltpu.VMEM((SPAN, n_res * A_WIDTH), BF16)] * 2,
        compiler_params=_params("parallel", "parallel", "arbitrary"),
        name=f"dilated_attention_d{dil}",
    )(view(q, A_WIDTH), view(k, A_WIDTH), view(v, A_WIDTH))
    return o.reshape(batch * seq, A_WIDTH), lse.reshape(batch * seq, LANES)


def _even_tail_kernel(x_ref, halo_ref, g_ref, w_ref, wpool_ref, pscale_ref, mk_ref, mv_ref,
                      wout_ref, o0_ref, o1_ref, o2_ref, l0_ref, l1_ref, l2_ref, out_ref,
                      h_ref, p_ref, y_ref, *, tiles_per_seq):
    seq_tile = pl.program_id(0) % tiles_per_seq
    _rmsnorm_rows(x_ref, g_ref, h_ref)
    _project(h_ref, w_ref, p_ref)
    z0 = B_WIDTH + M_WIDTH

    h_halo = _rmsnorm_bf16(halo_ref[...], g_ref[...])
    xb_halo = _dot(h_halo, w_ref[:, 0:B_WIDTH]) * (seq_tile > 0).astype(F32)
    run = jnp.concatenate([xb_halo, p_ref[:, 0:B_WIDTH]], axis=0)
    t = seq_tile * ROW_TILE + lax.broadcasted_iota(jnp.int32, (ROW_TILE, POOL_CH), 0)
    shift = 1
    for gi, w in enumerate(POOL_SIZES):
        while shift < w:
            run = run + pltpu.roll(run, shift, 0)
            shift *= 2
        lanes = slice(gi * POOL_CH, (gi + 1) * POOL_CH)
        cnt = jnp.minimum(t + 1, w).astype(F32)
        pooled = run[POOL_HALO:, lanes] / cnt - p_ref[:, lanes]
        yb = _dot(pooled.astype(BF16), wpool_ref[gi]) * pscale_ref[:, lanes]
        zb = p_ref[:, z0 + A_WIDTH + gi * POOL_CH:z0 + A_WIDTH + (gi + 1) * POOL_CH]
        y_ref[:, A_WIDTH + gi * POOL_CH:A_WIDTH + (gi + 1) * POOL_CH] = (yb * _silu(zb)).astype(BF16)

    m_out = _memory_attention(p_ref, B_WIDTH, mk_ref.at[0, 0], mv_ref.at[0, 0])
    for hd in range(MEM_HEADS):
        c0 = z0 + A_WIDTH + B_WIDTH + hd * HEAD_DIM
        y0 = A_WIDTH + B_WIDTH + hd * HEAD_DIM
        y_ref[:, y0:y0 + HEAD_DIM] = (m_out[hd] * _silu(p_ref[:, c0:c0 + HEAD_DIM])).astype(BF16)

    lses = (l0_ref[...], l1_ref[...], l2_ref[...])
    lmax = jnp.maximum(jnp.maximum(lses[0], lses[1]), lses[2])
    es = [jnp.exp(l - lmax) for l in lses]
    inv = 1.0 / (es[0] + es[1] + es[2])
    wts = [e * inv for e in es]
    o_refs = (o0_ref, o1_ref, o2_ref)
    for hd in range(A_HEADS):
        lanes = slice(hd * HEAD_DIM, (hd + 1) * HEAD_DIM)
        a = sum(wts[gi][:, hd:hd + 1] * o_refs[gi][:, lanes].astype(F32) for gi in range(N_GROUPS))
        za = p_ref[:, z0 + hd * HEAD_DIM:z0 + (hd + 1) * HEAD_DIM]
        y_ref[:, lanes] = (a * _silu(za)).astype(BF16)

    for c in range(0, D_MODEL, COL_CHUNK):
        cols = slice(c, c + COL_CHUNK)
        out_ref[:, cols] = x_ref[:, cols] + _dot(y_ref[...], wout_ref[:, cols])


def _even_tail(x2, g, w_rest, w_pool, pool_scale, memkv, layer, w_out, outs, lses, batch, seq):
    rows = x2.shape[0]
    tiles_per_seq = seq // ROW_TILE
    halo_blocks = ROW_TILE // POOL_HALO
    row_spec = lambda w: pl.BlockSpec((ROW_TILE, w), lambda i: (i, 0))
    halo_spec = pl.BlockSpec((POOL_HALO, D_MODEL), lambda i: (jnp.maximum(i * halo_blocks - 1, 0), 0))
    mk_spec = pl.BlockSpec((1, 1, N_MEM, M_WIDTH), lambda i: (layer, i // tiles_per_seq, 0, 0))
    mv_spec = pl.BlockSpec((1, 1, N_MEM, M_WIDTH), lambda i: (layer, i // tiles_per_seq, 0, 1))
    return pl.pallas_call(
        functools.partial(_even_tail_kernel, tiles_per_seq=tiles_per_seq),
        grid=(rows // ROW_TILE,),
        in_specs=[row_spec(D_MODEL), halo_spec, _const_spec((1, D_MODEL)),
                  _const_spec((D_MODEL, EVEN_REST)), _const_spec((len(POOL_SIZES), POOL_CH, POOL_CH)),
                  _const_spec((1, B_WIDTH)), mk_spec, mv_spec, _const_spec((EVEN_MIX, D_MODEL)),
                  row_spec(A_WIDTH), row_spec(A_WIDTH), row_spec(A_WIDTH),
                  row_spec(LANES), row_spec(LANES), row_spec(LANES)],
        out_specs=row_spec(D_MODEL),
        out_shape=jax.ShapeDtypeStruct((rows, D_MODEL), F32),
        scratch_shapes=[pltpu.VMEM((ROW_TILE, D_MODEL), BF16), pltpu.VMEM((ROW_TILE, EVEN_REST), F32),
                        pltpu.VMEM((ROW_TILE, EVEN_MIX), BF16)],
        compiler_params=_params("parallel"),
        name="even_tail",
    )(x2, x2, g.reshape(1, D_MODEL), w_rest, w_pool, pool_scale.reshape(1, B_WIDTH),
      memkv, memkv, w_out, *outs, *lses)


def _even_layer(x2, tabs, memkv, layer, g, w_in, w_pool, pool_scale, w_out, batch, seq):
    w_in = w_in.astype(BF16)
    pieces = _qkv_project(x2, g, w_in[:, :QKV_WIDTH], tabs)
    v = pieces[2 * N_GROUPS]
    outs, lses = [], []
    for gi, (_, dil) in enumerate(DIL_GROUPS):
        o, lse = _dilated_attention(pieces[2 * gi], pieces[2 * gi + 1], v, batch, seq, dil)
        outs.append(o)
        lses.append(lse)
    return _even_tail(x2, g, w_in[:, QKV_WIDTH:], w_pool.astype(BF16), pool_scale, memkv, layer,
                      w_out.astype(BF16), outs, lses, batch, seq)


def _odd_kernel(x_ref, g_ref, w_ref, lng_ref, lnb_ref, ws_ref, bs_ref, mk_ref, mv_ref, wout_ref,
                fg_ref, out_ref, h_ref, p_ref, vn_ref, y_ref, *, final_norm):
    _rmsnorm_rows(x_ref, g_ref, h_ref)
    _project(h_ref, w_ref, p_ref)
    z0 = 2 * C_WIDTH + M_WIDTH

    for r in range(0, ROW_TILE, ROW_BAND):
        v = p_ref[r:r + ROW_BAND, C_WIDTH:2 * C_WIDTH]
        vc = v - jnp.mean(v, axis=-1, keepdims=True)
        var = jnp.mean(vc * vc, axis=-1, keepdims=True)
        vn_ref[r:r + ROW_BAND, :] = ((vc * lax.rsqrt(var + EPS)) * lng_ref[...] + lnb_ref[...]).astype(BF16)

    tri_r = lax.broadcasted_iota(jnp.int32, (CHUNK, CHUNK), 0)
    tri_c = lax.broadcasted_iota(jnp.int32, (CHUNK, CHUNK), 1)
    causal = tri_c <= tri_r
    for gi in range(C_GROUPS):
        lanes = slice(gi * C_CH, (gi + 1) * C_CH)
        ws = jnp.where(causal, ws_ref[gi], jnp.zeros((CHUNK, CHUNK), BF16))
        bias = bs_ref[:, gi:gi + 1]
        for n in range(ROW_TILE // CHUNK):
            rows = slice(n * CHUNK, (n + 1) * CHUNK)
            mixed = _dot(ws, vn_ref[rows, lanes]) + bias
            gate = _silu(p_ref[rows, z0 + gi * C_CH:z0 + (gi + 1) * C_CH])
            y_ref[rows, lanes] = (p_ref[rows, lanes] * mixed * gate).astype(BF16)

    m_out = _memory_attention(p_ref, 2 * C_WIDTH, mk_ref.at[0, 0], mv_ref.at[0, 0])
    for hd in range(MEM_HEADS):
        c0 = z0 + C_WIDTH + hd * HEAD_DIM
        y0 = C_WIDTH + hd * HEAD_DIM
        y_ref[:, y0:y0 + HEAD_DIM] = (m_out[hd] * _silu(p_ref[:, c0:c0 + HEAD_DIM])).astype(BF16)

    res_ref = p_ref if final_norm else out_ref
    for c in range(0, D_MODEL, COL_CHUNK):
        cols = slice(c, c + COL_CHUNK)
        res_ref[:, cols] = x_ref[:, cols] + _dot(y_ref[...], wout_ref[:, cols])
    if final_norm:
        for r in range(0, ROW_TILE, ROW_BAND):
            res = p_ref[r:r + ROW_BAND, 0:D_MODEL]
            ms = jnp.mean(res * res, axis=-1, keepdims=True)
            out_ref[r:r + ROW_BAND, :] = (res * lax.rsqrt(ms + EPS)) * fg_ref[...]


def _odd_layer(x2, memkv, layer, g, w_in, ln_g, ln_b, w_s, b_s, w_out, final_g, final_norm, seq):
    rows = x2.shape[0]
    tiles_per_seq = seq // ROW_TILE
    row_spec = pl.BlockSpec((ROW_TILE, D_MODEL), lambda i: (i, 0))
    mk_spec = pl.BlockSpec((1, 1, N_MEM, M_WIDTH), lambda i: (layer, i // tiles_per_seq, 0, 0))
    mv_spec = pl.BlockSpec((1, 1, N_MEM, M_WIDTH), lambda i: (layer, i // tiles_per_seq, 0, 1))
    return pl.pallas_call(
        functools.partial(_odd_kernel, final_norm=final_norm),
        grid=(rows // ROW_TILE,),
        in_specs=[row_spec, _const_spec((1, D_MODEL)), _const_spec((D_MODEL, ODD_IN)),
                  _const_spec((1, C_WIDTH)), _const_spec((1, C_WIDTH)),
                  _const_spec((C_GROUPS, CHUNK, CHUNK)), _const_spec((CHUNK, C_GROUPS)),
                  mk_spec, mv_spec, _const_spec((ODD_MIX, D_MODEL)), _const_spec((1, D_MODEL))],
        out_specs=row_spec,
        out_shape=jax.ShapeDtypeStruct((rows, D_MODEL), F32),
        scratch_shapes=[pltpu.VMEM((ROW_TILE, D_MODEL), BF16), pltpu.VMEM((ROW_TILE, ODD_IN), F32),
                        pltpu.VMEM((ROW_TILE, C_WIDTH), BF16), pltpu.VMEM((ROW_TILE, ODD_MIX), BF16)],
        compiler_params=_params("parallel"),
        name="odd_layer",
    )(x2, g.reshape(1, D_MODEL), w_in.astype(BF16), ln_g.reshape(1, C_WIDTH), ln_b.reshape(1, C_WIDTH),
      w_s.astype(BF16), b_s.T, memkv, memkv, w_out.astype(BF16), final_g.reshape(1, D_MODEL))


def kernel(x, mem, positions, g_mem, even_norm_g, even_w_in, even_w_pool, even_pool_scale,
           even_w_mem_kv, even_w_out, odd_norm_g, odd_w_in, odd_ln_g, odd_ln_b, odd_w_s,
           odd_b_s, odd_w_mem_kv, odd_w_out, final_norm_g):
    batch, seq, _ = x.shape
    depth = even_norm_g.shape[0] + odd_norm_g.shape[0]
    assert seq % (ROW_TILE) == 0 and seq % (max(d for _, d in DIL_GROUPS) * SPAN) == 0

    tabs = _rope_tables(positions)
    w_kv = jnp.stack([(even_w_mem_kv if l % 2 == 0 else odd_w_mem_kv)[l // 2] for l in range(depth)])
    memkv = _memory_kv(mem, g_mem, w_kv.astype(BF16))

    x2 = x.reshape(batch * seq, D_MODEL)
    for layer in range(depth):
        i = layer // 2
        if layer % 2 == 0:
            x2 = _even_layer(x2, tabs, memkv, layer, even_norm_g[i], even_w_in[i], even_w_pool[i],
                             even_pool_scale[i], even_w_out[i], batch, seq)
        else:
            x2 = _odd_layer(x2, memkv, layer, odd_norm_g[i], odd_w_in[i], odd_ln_g[i], odd_ln_b[i],
                            odd_w_s[i], odd_b_s[i], odd_w_out[i], final_norm_g,
                            layer == depth - 1, seq)
    return x2.reshape(batch, seq, D_MODEL)
```

```python
import functools

import jax
import jax.numpy as jnp
from jax import lax
from jax.experimental import pallas as pl
from jax.experimental.pallas import tpu as pltpu

D_MODEL = 1024
HEAD_DIM = 128
EPS = 1e-6
ROPE_THETA = 500000.0
ROT_DIM = HEAD_DIM // 4
ROT_HALF = ROT_DIM // 2
N_MEM = 256
MEM_HEADS = 4
DIL_GROUPS = ((128, 1), (512, 4), (2048, 16))
A_HEADS = 4
POOL_SIZES = (2, 4, 8, 16)
POOL_CH = 128
CHUNK = 128
C_GROUPS = 8
C_CH = 128

A_WIDTH = A_HEADS * HEAD_DIM
B_WIDTH = len(POOL_SIZES) * POOL_CH
C_WIDTH = C_GROUPS * C_CH
M_WIDTH = MEM_HEADS * HEAD_DIM
EVEN_MIX = A_WIDTH + B_WIDTH + M_WIDTH
ODD_MIX = C_WIDTH + M_WIDTH
N_GROUPS = len(DIL_GROUPS)
A_QK_WIDTH = 2 * N_GROUPS * A_WIDTH
QKV_WIDTH = A_QK_WIDTH + A_WIDTH
EVEN_IN = QKV_WIDTH + B_WIDTH + M_WIDTH + EVEN_MIX
EVEN_REST = EVEN_IN - QKV_WIDTH
ODD_IN = 2 * C_WIDTH + M_WIDTH + ODD_MIX
SPAN = 128
SCALE = HEAD_DIM ** -0.5
POOL_HALO = max(POOL_SIZES)
SUPER = max(d for _, d in DIL_GROUPS) * SPAN
MIX_BAND = 256

LANES = 128
ROW_TILE = 512
ROW_BAND = 128
COL_CHUNK = 512
VMEM_LIMIT = 56 * 1024 * 1024

F32 = jnp.float32
BF16 = jnp.bfloat16


def _const_spec(shape):
    nd = len(shape)
    return pl.BlockSpec(shape, lambda *_: (0,) * nd, pipeline_mode=pl.Buffered(1))


def _params(*sem):
    return pltpu.CompilerParams(dimension_semantics=sem, vmem_limit_bytes=VMEM_LIMIT)


def _rmsnorm_bf16(x, g):
    ms = jnp.mean(x * x, axis=-1, keepdims=True)
    return ((x * lax.rsqrt(ms + EPS)) * g).astype(BF16)


def _rmsnorm_rows(x_ref, g_ref, h_ref):
    g = g_ref[...]
    for r in range(0, x_ref.shape[0], ROW_BAND):
        h_ref[r:r + ROW_BAND, :] = _rmsnorm_bf16(x_ref[r:r + ROW_BAND, :], g)


def _project(h_ref, w_ref, p_ref, col0=0):
    for c in range(0, p_ref.shape[1], COL_CHUNK):
        p_ref[:, c:c + COL_CHUNK] = _dot(h_ref[...], w_ref[:, col0 + c:col0 + c + COL_CHUNK])


def _dot(a, b):
    return jnp.dot(a, b, preferred_element_type=F32)


def _dot_nt(a, b):
    return lax.dot_general(a, b, (((1,), (1,)), ((), ())), preferred_element_type=F32)


def _silu(z):
    return z * (1.0 / (1.0 + jnp.exp(-z)))


def _rope_kernel(pos_ref, inv_ref, cos_ref, sin_ref):
    ang = pos_ref[...].astype(F32) * inv_ref[...]
    cos_ref[...] = jnp.cos(ang)
    sin_ref[...] = jnp.sin(ang)


def _rope_tables(positions):
    n = positions.size
    per_row = LANES // ROT_HALF
    inv = ROPE_THETA ** (-jnp.arange(0, ROT_DIM, 2, dtype=F32) / ROT_DIM)
    pos_c = jnp.repeat(positions.reshape(n // per_row, per_row), ROT_HALF, axis=1)
    inv_c = jnp.tile(inv, per_row).reshape(1, LANES)
    rows = n // per_row
    cos_c, sin_c = pl.pallas_call(
        _rope_kernel,
        out_shape=(jax.ShapeDtypeStruct((rows, LANES), F32),) * 2,
        name="rope_tables",
    )(pos_c, inv_c)
    cos = cos_c.reshape(n, ROT_HALF)
    sin = sin_c.reshape(n, ROT_HALF)
    ones = jnp.ones((n, LANES - ROT_DIM), F32)
    zeros = jnp.zeros((n, LANES - ROT_DIM), F32)
    zhalf = jnp.zeros((n, ROT_HALF), F32)
    c_tab = jnp.concatenate([cos, cos, ones], axis=1)
    sa_tab = jnp.concatenate([zhalf, sin, zeros], axis=1)
    sb_tab = jnp.concatenate([sin, zhalf, zeros], axis=1)
    return c_tab, sa_tab, sb_tab


def _memkv_kernel(mem_ref, g_ref, w_ref, out_ref):
    mem_n = _rmsnorm_bf16(mem_ref[0], g_ref[...])
    out_ref[0, 0] = _dot(mem_n, w_ref[0]).astype(BF16)


def _memory_kv(mem, g_mem, w_kv):
    n_layers = w_kv.shape[0]
    batch = mem.shape[0]
    return pl.pallas_call(
        _memkv_kernel,
        grid=(n_layers, batch),
        in_specs=[
            pl.BlockSpec((1, N_MEM, D_MODEL), lambda l, b: (b, 0, 0)),
            pl.BlockSpec((1, D_MODEL), lambda l, b: (0, 0)),
            pl.BlockSpec((1, D_MODEL, 2 * M_WIDTH), lambda l, b: (l, 0, 0)),
        ],
        out_specs=pl.BlockSpec((1, 1, N_MEM, 2 * M_WIDTH), lambda l, b: (l, b, 0, 0)),
        out_shape=jax.ShapeDtypeStruct((n_layers, batch, N_MEM, 2 * M_WIDTH), BF16),
        compiler_params=_params("parallel", "parallel"),
        name="memory_kv",
    )(mem, g_mem.reshape(1, D_MODEL), w_kv)


def _memory_attention(p_ref, col0, mk_ref, mv_ref):
    outs = []
    for h in range(MEM_HEADS):
        lanes = slice(h * HEAD_DIM, (h + 1) * HEAD_DIM)
        q = p_ref[:, col0 + h * HEAD_DIM:col0 + (h + 1) * HEAD_DIM].astype(BF16)
        sc = _dot_nt(q, mk_ref[:, lanes]) * SCALE
        e = jnp.exp(sc - jnp.max(sc, axis=-1, keepdims=True))
        p = e / jnp.sum(e, axis=-1, keepdims=True)
        outs.append(_dot(p.astype(BF16), mv_ref[:, lanes]))
    return outs


def _qkv_kernel(x_ref, g_ref, w_ref, c_ref, sa_ref, sb_ref, *refs):
    n_qk = 2 * N_GROUPS
    qk_refs, v_refs = refs[:n_qk], refs[n_qk:n_qk + N_GROUPS]
    h_ref, st_ref = refs[n_qk + N_GROUPS:]
    _rmsnorm_rows(x_ref, g_ref, h_ref)
    c_tab, sa_tab, sb_tab = c_ref[...], sa_ref[...], sb_ref[...]

    def emit(blk, stage, head, targets):
        staged = False
        for ref, dil in targets:
            if dil == 1:
                ref[0, head, 0] = blk.astype(BF16)
                continue
            if not staged:
                stage[...] = blk
                staged = True
            for r in range(dil):
                ref[0, head, r] = stage[pl.ds(r, ROW_TILE // dil, stride=dil), :].astype(BF16)

    for c in range(QKV_WIDTH // COL_CHUNK):
        acc = _dot(h_ref[...], w_ref[:, c * COL_CHUNK:(c + 1) * COL_CHUNK])
        for hh in range(COL_CHUNK // HEAD_DIM):
            blk = acc[:, hh * HEAD_DIM:(hh + 1) * HEAD_DIM]
            if c < n_qk:
                blk = (blk * c_tab
                       + pltpu.roll(blk, ROT_HALF, 1) * sa_tab
                       - pltpu.roll(blk, HEAD_DIM - ROT_HALF, 1) * sb_tab)
                targets = [(qk_refs[c], DIL_GROUPS[c // 2][1])]
            else:
                targets = [(v_refs[gi], DIL_GROUPS[gi][1]) for gi in range(N_GROUPS)]
            emit(blk, st_ref.at[hh], hh, targets)


def _qkv_project(x2, g, w_qkv, tabs, batch, seq):
    rows = x2.shape[0]
    tiles_per_seq = seq // ROW_TILE
    row_spec = lambda w: pl.BlockSpec((ROW_TILE, w), lambda i: (i, 0))
    dils = [DIL_GROUPS[c // 2][1] for c in range(2 * N_GROUPS)] + [d for _, d in DIL_GROUPS]
    out_spec = lambda d: pl.BlockSpec((1, A_HEADS, d, ROW_TILE // d, HEAD_DIM),
                                      lambda i: (i // tiles_per_seq, 0, 0, i % tiles_per_seq, 0))
    out_shape = lambda d: jax.ShapeDtypeStruct((batch, A_HEADS, d, seq // d, HEAD_DIM), BF16)
    return pl.pallas_call(
        _qkv_kernel,
        grid=(rows // ROW_TILE,),
        in_specs=[row_spec(D_MODEL), _const_spec((1, D_MODEL)), _const_spec((D_MODEL, QKV_WIDTH)),
                  row_spec(LANES), row_spec(LANES), row_spec(LANES)],
        out_specs=[out_spec(d) for d in dils],
        out_shape=[out_shape(d) for d in dils],
        scratch_shapes=[pltpu.VMEM((ROW_TILE, D_MODEL), BF16),
                        pltpu.VMEM((COL_CHUNK // HEAD_DIM, ROW_TILE, HEAD_DIM), F32)],
        compiler_params=_params("parallel"),
        name="even_qkv",
    )(x2, g.reshape(1, D_MODEL), w_qkv, *tabs)


def _mixer_kernel(q0, k0, v0, q1, k1, v1, q2, k2, v2, out_ref,
                  ck0, cv0, ck1, cv1, ck2, cv2, o_acc, l_acc):
    tile = pl.program_id(2)
    carries = (ck0, cv0, ck1, cv1, ck2, cv2)

    @pl.when(tile == 0)
    def _():
        for c in carries:
            c[...] = jnp.zeros_like(c)

    no_prev = jnp.where(tile > 0, 0.0, -jnp.inf).astype(F32)
    row = lax.broadcasted_iota(jnp.int32, (SPAN, SPAN), 0)
    col = lax.broadcasted_iota(jnp.int32, (SPAN, SPAN), 1)
    tri_prev = col >= row
    tri_cur = col <= row

    def attend(q, kk, vv, first):
        s = _dot_nt(q, kk) * SCALE
        sp, sc = s[:, :SPAN], s[:, SPAN:]
        if first:
            sp = sp + no_prev
        sp = jnp.where(tri_prev, sp, -jnp.inf)
        sc = jnp.where(tri_cur, sc, -jnp.inf)
        m = jnp.maximum(jnp.max(sp, axis=-1, keepdims=True), jnp.max(sc, axis=-1, keepdims=True))
        pp, pc = jnp.exp(sp - m), jnp.exp(sc - m)
        den = jnp.sum(pp, axis=-1, keepdims=True) + jnp.sum(pc, axis=-1, keepdims=True)
        num = _dot(jnp.concatenate([pp, pc], axis=1).astype(BF16), vv)
        return num * (1.0 / den), jnp.broadcast_to(m + jnp.log(den), (SPAN, LANES))

    cat = lambda a, b: jnp.concatenate([a, b], axis=0)

    o, l = attend(q0[0, 0, 0, 0:SPAN, :], cat(ck0[...], k0[0, 0, 0, 0:SPAN, :]),
                  cat(cv0[...], v0[0, 0, 0, 0:SPAN, :]), True)
    o_acc[0, 0:SPAN, :] = o
    l_acc[0, 0:SPAN, :] = l
    n_blk0 = SUPER // SPAN
    unroll0 = 3

    def body0(it, carry):
        for s in range(unroll0):
            r0 = pl.multiple_of((1 + unroll0 * it + s) * SPAN, SPAN)
            both = pl.ds(r0 - SPAN, 2 * SPAN)
            o, l = attend(q0[0, 0, 0, pl.ds(r0, SPAN), :], k0[0, 0, 0, both, :], v0[0, 0, 0, both, :], False)
            o_acc[0, pl.ds(r0, SPAN), :] = o
            l_acc[0, pl.ds(r0, SPAN), :] = l
        return carry

    lax.fori_loop(0, (n_blk0 - 1) // unroll0, body0, 0)
    ck0[...] = k0[0, 0, 0, SUPER - SPAN:SUPER, :]
    cv0[...] = v0[0, 0, 0, SUPER - SPAN:SUPER, :]

    dil1 = DIL_GROUPS[1][1]
    n_blk1 = SUPER // (dil1 * SPAN)

    def body1(rho, carry):
        for jb in range(n_blk1):
            rows = slice(jb * SPAN, (jb + 1) * SPAN)
            if jb == 0:
                kk, vv = cat(ck1[rho], k1[0, 0, rho, rows, :]), cat(cv1[rho], v1[0, 0, rho, rows, :])
            else:
                both = slice((jb - 1) * SPAN, (jb + 1) * SPAN)
                kk, vv = k1[0, 0, rho, both, :], v1[0, 0, rho, both, :]
            o, l = attend(q1[0, 0, rho, rows, :], kk, vv, jb == 0)
            dst = pl.ds(jb * dil1 * SPAN + rho, SPAN, stride=dil1)
            o_acc[1, dst, :] = o
            l_acc[1, dst, :] = l
        last = slice((n_blk1 - 1) * SPAN, n_blk1 * SPAN)
        ck1[rho] = k1[0, 0, rho, last, :]
        cv1[rho] = v1[0, 0, rho, last, :]
        return carry

    lax.fori_loop(0, dil1, body1, 0)

    dil2 = DIL_GROUPS[2][1]
    unroll2 = 4

    def body2(it, carry):
        for s in range(unroll2):
            rho = unroll2 * it + s
            o, l = attend(q2[0, 0, rho], cat(ck2[rho], k2[0, 0, rho]), cat(cv2[rho], v2[0, 0, rho]), True)
            dst = pl.ds(rho, SPAN, stride=dil2)
            o_acc[2, dst, :] = o
            l_acc[2, dst, :] = l
            ck2[rho] = k2[0, 0, rho]
            cv2[rho] = v2[0, 0, rho]
        return carry

    lax.fori_loop(0, dil2 // unroll2, body2, 0)

    def mix(i, carry):
        rows = pl.ds(pl.multiple_of(i * MIX_BAND, MIX_BAND), MIX_BAND)
        ls = [l_acc[gi, rows, :] for gi in range(N_GROUPS)]
        lmax = jnp.maximum(jnp.maximum(ls[0], ls[1]), ls[2])
        es = [jnp.exp(l - lmax) for l in ls]
        acc = es[0] * o_acc[0, rows, :] + es[1] * o_acc[1, rows, :] + es[2] * o_acc[2, rows, :]
        out_ref[0, 0, rows, :] = (acc * (1.0 / (es[0] + es[1] + es[2]))).astype(BF16)
        return carry

    lax.fori_loop(0, SUPER // MIX_BAND, mix, 0)


def _dilated_mixer(pieces, batch, seq):
    ins, specs, carries = [], [], []
    for gi, (_, dil) in enumerate(DIL_GROUPS):
        spec = pl.BlockSpec((1, 1, dil, SUPER // dil, HEAD_DIM), lambda b, h, t: (b, h, 0, t, 0))
        ins += [pieces[2 * gi], pieces[2 * gi + 1], pieces[2 * N_GROUPS + gi]]
        specs += [spec] * 3
        shape = (SPAN, HEAD_DIM) if dil == 1 else (dil, SPAN, HEAD_DIM)
        carries += [pltpu.VMEM(shape, BF16)] * 2
    return pl.pallas_call(
        _mixer_kernel,
        grid=(batch, A_HEADS, seq // SUPER),
        in_specs=specs,
        out_specs=pl.BlockSpec((1, 1, SUPER, HEAD_DIM), lambda b, h, t: (b, h, t, 0)),
        out_shape=jax.ShapeDtypeStruct((batch, A_HEADS, seq, HEAD_DIM), BF16),
        scratch_shapes=carries + [pltpu.VMEM((N_GROUPS, SUPER, HEAD_DIM), F32)] * 2,
        compiler_params=_params("parallel", "parallel", "arbitrary"),
        name="dilated_mixer",
    )(*ins)


def _even_tail_kernel(x_ref, halo_ref, g_ref, w_ref, wpool_ref, pscale_ref, mk_ref, mv_ref,
                      wout_ref, a_ref, out_ref, h_ref, p_ref, y_ref, *, tiles_per_seq):
    seq_tile = pl.program_id(0) % tiles_per_seq
    _rmsnorm_rows(x_ref, g_ref, h_ref)
    _project(h_ref, w_ref, p_ref)
    z0 = B_WIDTH + M_WIDTH

    h_halo = _rmsnorm_bf16(halo_ref[...], g_ref[...])
    xb_halo = _dot(h_halo, w_ref[:, 0:B_WIDTH]) * (seq_tile > 0).astype(F32)
    run = jnp.concatenate([xb_halo, p_ref[:, 0:B_WIDTH]], axis=0)
    t = seq_tile * ROW_TILE + lax.broadcasted_iota(jnp.int32, (ROW_TILE, POOL_CH), 0)
    shift = 1
    for gi, w in enumerate(POOL_SIZES):
        while shift < w:
            run = run + pltpu.roll(run, shift, 0)
            shift *= 2
        lanes = slice(gi * POOL_CH, (gi + 1) * POOL_CH)
        cnt = jnp.minimum(t + 1, w).astype(F32)
        pooled = run[POOL_HALO:, lanes] / cnt - p_ref[:, lanes]
        yb = _dot(pooled.astype(BF16), wpool_ref[gi]) * pscale_ref[:, lanes]
        zb = p_ref[:, z0 + A_WIDTH + gi * POOL_CH:z0 + A_WIDTH + (gi + 1) * POOL_CH]
        y_ref[:, A_WIDTH + gi * POOL_CH:A_WIDTH + (gi + 1) * POOL_CH] = (yb * _silu(zb)).astype(BF16)

    m_out = _memory_attention(p_ref, B_WIDTH, mk_ref.at[0, 0], mv_ref.at[0, 0])
    for hd in range(MEM_HEADS):
        c0 = z0 + A_WIDTH + B_WIDTH + hd * HEAD_DIM
        y0 = A_WIDTH + B_WIDTH + hd * HEAD_DIM
        y_ref[:, y0:y0 + HEAD_DIM] = (m_out[hd] * _silu(p_ref[:, c0:c0 + HEAD_DIM])).astype(BF16)

    for hd in range(A_HEADS):
        lanes = slice(hd * HEAD_DIM, (hd + 1) * HEAD_DIM)
        za = p_ref[:, z0 + hd * HEAD_DIM:z0 + (hd + 1) * HEAD_DIM]
        y_ref[:, lanes] = (a_ref[0, hd].astype(F32) * _silu(za)).astype(BF16)

    for c in range(0, D_MODEL, COL_CHUNK):
        cols = slice(c, c + COL_CHUNK)
        out_ref[:, cols] = x_ref[:, cols] + _dot(y_ref[...], wout_ref[:, cols])


def _even_tail(x2, g, w_rest, w_pool, pool_scale, memkv, layer, w_out, a_out, batch, seq):
    rows = x2.shape[0]
    tiles_per_seq = seq // ROW_TILE
    halo_blocks = ROW_TILE // POOL_HALO
    row_spec = lambda w: pl.BlockSpec((ROW_TILE, w), lambda i: (i, 0))
    halo_spec = pl.BlockSpec((POOL_HALO, D_MODEL), lambda i: (jnp.maximum(i * halo_blocks - 1, 0), 0))
    mk_spec = pl.BlockSpec((1, 1, N_MEM, M_WIDTH), lambda i: (layer, i // tiles_per_seq, 0, 0))
    mv_spec = pl.BlockSpec((1, 1, N_MEM, M_WIDTH), lambda i: (layer, i // tiles_per_seq, 0, 1))
    return pl.pallas_call(
        functools.partial(_even_tail_kernel, tiles_per_seq=tiles_per_seq),
        grid=(rows // ROW_TILE,),
        in_specs=[row_spec(D_MODEL), halo_spec, _const_spec((1, D_MODEL)),
                  _const_spec((D_MODEL, EVEN_REST)), _const_spec((len(POOL_SIZES), POOL_CH, POOL_CH)),
                  _const_spec((1, B_WIDTH)), mk_spec, mv_spec, _const_spec((EVEN_MIX, D_MODEL)),
                  pl.BlockSpec((1, A_HEADS, ROW_TILE, HEAD_DIM),
                               lambda i: (i // tiles_per_seq, 0, i % tiles_per_seq, 0))],
        out_specs=row_spec(D_MODEL),
        out_shape=jax.ShapeDtypeStruct((rows, D_MODEL), F32),
        scratch_shapes=[pltpu.VMEM((ROW_TILE, D_MODEL), BF16), pltpu.VMEM((ROW_TILE, EVEN_REST), F32),
                        pltpu.VMEM((ROW_TILE, EVEN_MIX), BF16)],
        compiler_params=_params("parallel"),
        name="even_tail",
    )(x2, x2, g.reshape(1, D_MODEL), w_rest, w_pool, pool_scale.reshape(1, B_WIDTH),
      memkv, memkv, w_out, a_out)


def _even_layer(x2, tabs, memkv, layer, g, w_in, w_pool, pool_scale, w_out, batch, seq):
    w_in = w_in.astype(BF16)
    pieces = _qkv_project(x2, g, w_in[:, :QKV_WIDTH], tabs, batch, seq)
    a_out = _dilated_mixer(pieces, batch, seq)
    return _even_tail(x2, g, w_in[:, QKV_WIDTH:], w_pool.astype(BF16), pool_scale, memkv, layer,
                      w_out.astype(BF16), a_out, batch, seq)


def _odd_kernel(x_ref, g_ref, w_ref, lng_ref, lnb_ref, ws_ref, bs_ref, mk_ref, mv_ref, wout_ref,
                fg_ref, out_ref, h_ref, p_ref, vn_ref, y_ref, *, final_norm):
    _rmsnorm_rows(x_ref, g_ref, h_ref)
    _project(h_ref, w_ref, p_ref)
    z0 = 2 * C_WIDTH + M_WIDTH

    for r in range(0, ROW_TILE, ROW_BAND):
        v = p_ref[r:r + ROW_BAND, C_WIDTH:2 * C_WIDTH]
        vc = v - jnp.mean(v, axis=-1, keepdims=True)
        var = jnp.mean(vc * vc, axis=-1, keepdims=True)
        vn_ref[r:r + ROW_BAND, :] = ((vc * lax.rsqrt(var + EPS)) * lng_ref[...] + lnb_ref[...]).astype(BF16)

    tri_r = lax.broadcasted_iota(jnp.int32, (CHUNK, CHUNK), 0)
    tri_c = lax.broadcasted_iota(jnp.int32, (CHUNK, CHUNK), 1)
    causal = tri_c <= tri_r
    for gi in range(C_GROUPS):
        lanes = slice(gi * C_CH, (gi + 1) * C_CH)
        ws = jnp.where(causal, ws_ref[gi], jnp.zeros((CHUNK, CHUNK), BF16))
        bias = bs_ref[:, gi:gi + 1]
        for n in range(ROW_TILE // CHUNK):
            rows = slice(n * CHUNK, (n + 1) * CHUNK)
            mixed = _dot(ws, vn_ref[rows, lanes]) + bias
            gate = _silu(p_ref[rows, z0 + gi * C_CH:z0 + (gi + 1) * C_CH])
            y_ref[rows, lanes] = (p_ref[rows, lanes] * mixed * gate).astype(BF16)

    m_out = _memory_attention(p_ref, 2 * C_WIDTH, mk_ref.at[0, 0], mv_ref.at[0, 0])
    for hd in range(MEM_HEADS):
        c0 = z0 + C_WIDTH + hd * HEAD_DIM
        y0 = C_WIDTH + hd * HEAD_DIM
        y_ref[:, y0:y0 + HEAD_DIM] = (m_out[hd] * _silu(p_ref[:, c0:c0 + HEAD_DIM])).astype(BF16)

    res_ref = p_ref if final_norm else out_ref
    for c in range(0, D_MODEL, COL_CHUNK):
        cols = slice(c, c + COL_CHUNK)
        res_ref[:, cols] = x_ref[:, cols] + _dot(y_ref[...], wout_ref[:, cols])
    if final_norm:
        for r in range(0, ROW_TILE, ROW_BAND):
            res = p_ref[r:r + ROW_BAND, 0:D_MODEL]
            ms = jnp.mean(res * res, axis=-1, keepdims=True)
            out_ref[r:r + ROW_BAND, :] = (res * lax.rsqrt(ms + EPS)) * fg_ref[...]


def _odd_layer(x2, memkv, layer, g, w_in, ln_g, ln_b, w_s, b_s, w_out, final_g, final_norm, seq):
    rows = x2.shape[0]
    tiles_per_seq = seq // ROW_TILE
    row_spec = pl.BlockSpec((ROW_TILE, D_MODEL), lambda i: (i, 0))
    mk_spec = pl.BlockSpec((1, 1, N_MEM, M_WIDTH), lambda i: (layer, i // tiles_per_seq, 0, 0))
    mv_spec = pl.BlockSpec((1, 1, N_MEM, M_WIDTH), lambda i: (layer, i // tiles_per_seq, 0, 1))
    return pl.pallas_call(
        functools.partial(_odd_kernel, final_norm=final_norm),
        grid=(rows // ROW_TILE,),
        in_specs=[row_spec, _const_spec((1, D_MODEL)), _const_spec((D_MODEL, ODD_IN)),
                  _const_spec((1, C_WIDTH)), _const_spec((1, C_WIDTH)),
                  _const_spec((C_GROUPS, CHUNK, CHUNK)), _const_spec((CHUNK, C_GROUPS)),
                  mk_spec, mv_spec, _const_spec((ODD_MIX, D_MODEL)), _const_spec((1, D_MODEL))],
        out_specs=row_spec,
        out_shape=jax.ShapeDtypeStruct((rows, D_MODEL), F32),
        scratch_shapes=[pltpu.VMEM((ROW_TILE, D_MODEL), BF16), pltpu.VMEM((ROW_TILE, ODD_IN), F32),
                        pltpu.VMEM((ROW_TILE, C_WIDTH), BF16), pltpu.VMEM((ROW_TILE, ODD_MIX), BF16)],
        compiler_params=_params("parallel"),
        name="odd_layer",
    )(x2, g.reshape(1, D_MODEL), w_in.astype(BF16), ln_g.reshape(1, C_WIDTH), ln_b.reshape(1, C_WIDTH),
      w_s.astype(BF16), b_s.T, memkv, memkv, w_out.astype(BF16), final_g.reshape(1, D_MODEL))


def kernel(x, mem, positions, g_mem, even_norm_g, even_w_in, even_w_pool, even_pool_scale,
           even_w_mem_kv, even_w_out, odd_norm_g, odd_w_in, odd_ln_g, odd_ln_b, odd_w_s,
           odd_b_s, odd_w_mem_kv, odd_w_out, final_norm_g):
    batch, seq, _ = x.shape
    depth = even_norm_g.shape[0] + odd_norm_g.shape[0]
    assert seq % (ROW_TILE) == 0 and seq % (max(d for _, d in DIL_GROUPS) * SPAN) == 0

    tabs = _rope_tables(positions)
    w_kv = jnp.stack([(even_w_mem_kv if l % 2 == 0 else odd_w_mem_kv)[l // 2] for l in range(depth)])
    memkv = _memory_kv(mem, g_mem, w_kv.astype(BF16))

    x2 = x.reshape(batch * seq, D_MODEL)
    for layer in range(depth):
        i = layer // 2
        if layer % 2 == 0:
            x2 = _even_layer(x2, tabs, memkv, layer, even_norm_g[i], even_w_in[i], even_w_pool[i],
                             even_pool_scale[i], even_w_out[i], batch, seq)
        else:
            x2 = _odd_layer(x2, memkv, layer, odd_norm_g[i], odd_w_in[i], odd_ln_g[i], odd_ln_b[i],
                            odd_w_s[i], odd_b_s[i], odd_w_out[i], final_norm_g,
                            layer == depth - 1, seq)
    return x2.reshape(batch, seq, D_MODEL)
```

```python
import functools

import jax
import jax.numpy as jnp
from jax import lax
from jax.experimental import pallas as pl
from jax.experimental.pallas import tpu as pltpu

D_MODEL = 1024
HEAD_DIM = 128
EPS = 1e-6
ROPE_THETA = 500000.0
ROT_DIM = HEAD_DIM // 4
ROT_HALF = ROT_DIM // 2
N_MEM = 256
MEM_HEADS = 4
DIL_GROUPS = ((128, 1), (512, 4), (2048, 16))
A_HEADS = 4
POOL_SIZES = (2, 4, 8, 16)
POOL_CH = 128
CHUNK = 128
C_GROUPS = 8
C_CH = 128

A_WIDTH = A_HEADS * HEAD_DIM
B_WIDTH = len(POOL_SIZES) * POOL_CH
C_WIDTH = C_GROUPS * C_CH
M_WIDTH = MEM_HEADS * HEAD_DIM
EVEN_MIX = A_WIDTH + B_WIDTH + M_WIDTH
ODD_MIX = C_WIDTH + M_WIDTH
N_GROUPS = len(DIL_GROUPS)
A_QK_WIDTH = 2 * N_GROUPS * A_WIDTH
QKV_WIDTH = A_QK_WIDTH + A_WIDTH
EVEN_IN = QKV_WIDTH + B_WIDTH + M_WIDTH + EVEN_MIX
EVEN_REST = EVEN_IN - QKV_WIDTH
ODD_IN = 2 * C_WIDTH + M_WIDTH + ODD_MIX
SPAN = 128
SCALE = HEAD_DIM ** -0.5
SCALE_LOG2 = SCALE * 1.4426950408889634
POOL_HALO = max(POOL_SIZES)
SUPER = max(d for _, d in DIL_GROUPS) * SPAN
MIX_BAND = 256
UNROLL_D1 = 15
UNROLL_D4 = 4
UNROLL_D16 = 16

LANES = 128
ROW_TILE = 512
ROW_BAND = 128
COL_CHUNK = 512
VMEM_LIMIT = 56 * 1024 * 1024

F32 = jnp.float32
BF16 = jnp.bfloat16


def _const_spec(shape):
    nd = len(shape)
    return pl.BlockSpec(shape, lambda *_: (0,) * nd, pipeline_mode=pl.Buffered(1))


def _params(*sem):
    return pltpu.CompilerParams(dimension_semantics=sem, vmem_limit_bytes=VMEM_LIMIT)


def _rmsnorm_bf16(x, g):
    ms = jnp.mean(x * x, axis=-1, keepdims=True)
    return ((x * lax.rsqrt(ms + EPS)) * g).astype(BF16)


def _rmsnorm_rows(x_ref, g_ref, h_ref):
    g = g_ref[...]
    for r in range(0, x_ref.shape[0], ROW_BAND):
        h_ref[r:r + ROW_BAND, :] = _rmsnorm_bf16(x_ref[r:r + ROW_BAND, :], g)


def _project(h_ref, w_ref, p_ref, col0=0):
    for c in range(0, p_ref.shape[1], COL_CHUNK):
        p_ref[:, c:c + COL_CHUNK] = _dot(h_ref[...], w_ref[:, col0 + c:col0 + c + COL_CHUNK])


def _unrolled_loop(lo, hi, unroll, fn):
    n = hi - lo
    assert n % unroll == 0
    if unroll == n:
        for i in range(lo, hi):
            fn(i)
        return

    def trip(t, carry):
        for s in range(unroll):
            fn(lo + t * unroll + s)
        return carry

    lax.fori_loop(0, n // unroll, trip, 0)


def _dot(a, b):
    return jnp.dot(a, b, preferred_element_type=F32)


def _dot_nt(a, b):
    return lax.dot_general(a, b, (((1,), (1,)), ((), ())), preferred_element_type=F32)


def _silu(z):
    return z * (1.0 / (1.0 + jnp.exp(-z)))


def _rope_kernel(pos_ref, inv_ref, cos_ref, sin_ref):
    ang = pos_ref[...].astype(F32) * inv_ref[...]
    cos_ref[...] = jnp.cos(ang)
    sin_ref[...] = jnp.sin(ang)


def _rope_tables(positions):
    n = positions.size
    per_row = LANES // ROT_HALF
    inv = ROPE_THETA ** (-jnp.arange(0, ROT_DIM, 2, dtype=F32) / ROT_DIM)
    pos_c = jnp.repeat(positions.reshape(n // per_row, per_row), ROT_HALF, axis=1)
    inv_c = jnp.tile(inv, per_row).reshape(1, LANES)
    rows = n // per_row
    cos_c, sin_c = pl.pallas_call(
        _rope_kernel,
        out_shape=(jax.ShapeDtypeStruct((rows, LANES), F32),) * 2,
        name="rope_tables",
    )(pos_c, inv_c)
    cos = cos_c.reshape(n, ROT_HALF)
    sin = sin_c.reshape(n, ROT_HALF)
    ones = jnp.ones((n, LANES - ROT_DIM), F32)
    zeros = jnp.zeros((n, LANES - ROT_DIM), F32)
    zhalf = jnp.zeros((n, ROT_HALF), F32)
    c_tab = jnp.concatenate([cos, cos, ones], axis=1)
    sa_tab = jnp.concatenate([zhalf, sin, zeros], axis=1)
    sb_tab = jnp.concatenate([sin, zhalf, zeros], axis=1)
    return c_tab, sa_tab, sb_tab


def _memkv_kernel(mem_ref, g_ref, w_ref, out_ref):
    mem_n = _rmsnorm_bf16(mem_ref[0], g_ref[...])
    out_ref[0, 0] = _dot(mem_n, w_ref[0]).astype(BF16)


def _memory_kv(mem, g_mem, w_kv):
    n_layers = w_kv.shape[0]
    batch = mem.shape[0]
    return pl.pallas_call(
        _memkv_kernel,
        grid=(n_layers, batch),
        in_specs=[
            pl.BlockSpec((1, N_MEM, D_MODEL), lambda l, b: (b, 0, 0)),
            pl.BlockSpec((1, D_MODEL), lambda l, b: (0, 0)),
            pl.BlockSpec((1, D_MODEL, 2 * M_WIDTH), lambda l, b: (l, 0, 0)),
        ],
        out_specs=pl.BlockSpec((1, 1, N_MEM, 2 * M_WIDTH), lambda l, b: (l, b, 0, 0)),
        out_shape=jax.ShapeDtypeStruct((n_layers, batch, N_MEM, 2 * M_WIDTH), BF16),
        compiler_params=_params("parallel", "parallel"),
        name="memory_kv",
    )(mem, g_mem.reshape(1, D_MODEL), w_kv)


def _memory_attention(p_ref, col0, mk_ref, mv_ref):
    outs = []
    for h in range(MEM_HEADS):
        lanes = slice(h * HEAD_DIM, (h + 1) * HEAD_DIM)
        q = p_ref[:, col0 + h * HEAD_DIM:col0 + (h + 1) * HEAD_DIM].astype(BF16)
        sc = _dot_nt(q, mk_ref[:, lanes]) * SCALE_LOG2
        e = jnp.exp2(sc - jnp.max(sc, axis=-1, keepdims=True))
        p = e / jnp.sum(e, axis=-1, keepdims=True)
        outs.append(_dot(p.astype(BF16), mv_ref[:, lanes]))
    return outs


def _qkv_kernel(x_ref, g_ref, w_ref, c_ref, sa_ref, sb_ref, *refs):
    n_qk = 2 * N_GROUPS
    qk_refs, v_refs = refs[:n_qk], refs[n_qk:n_qk + N_GROUPS]
    h_ref, st_ref = refs[n_qk + N_GROUPS:]
    _rmsnorm_rows(x_ref, g_ref, h_ref)
    c_tab, sa_tab, sb_tab = c_ref[...], sa_ref[...], sb_ref[...]

    def emit(blk, stage, head, targets):
        staged = False
        for ref, dil in targets:
            if dil == 1:
                ref[0, head, 0] = blk.astype(BF16)
                continue
            if not staged:
                stage[...] = blk
                staged = True
            for r in range(dil):
                ref[0, head, r] = stage[pl.ds(r, ROW_TILE // dil, stride=dil), :].astype(BF16)

    for c in range(QKV_WIDTH // COL_CHUNK):
        acc = _dot(h_ref[...], w_ref[:, c * COL_CHUNK:(c + 1) * COL_CHUNK])
        for hh in range(COL_CHUNK // HEAD_DIM):
            blk = acc[:, hh * HEAD_DIM:(hh + 1) * HEAD_DIM]
            if c < n_qk:
                blk = (blk * c_tab
                       + pltpu.roll(blk, ROT_HALF, 1) * sa_tab
                       - pltpu.roll(blk, HEAD_DIM - ROT_HALF, 1) * sb_tab)
                targets = [(qk_refs[c], DIL_GROUPS[c // 2][1])]
            else:
                targets = [(v_refs[gi], DIL_GROUPS[gi][1]) for gi in range(N_GROUPS)]
            emit(blk, st_ref.at[hh], hh, targets)


def _qkv_project(x2, g, w_qkv, tabs, batch, seq):
    rows = x2.shape[0]
    tiles_per_seq = seq // ROW_TILE
    row_spec = lambda w: pl.BlockSpec((ROW_TILE, w), lambda i: (i, 0))
    dils = [DIL_GROUPS[c // 2][1] for c in range(2 * N_GROUPS)] + [d for _, d in DIL_GROUPS]
    out_spec = lambda d: pl.BlockSpec((1, A_HEADS, d, ROW_TILE // d, HEAD_DIM),
                                      lambda i: (i // tiles_per_seq, 0, 0, i % tiles_per_seq, 0))
    out_shape = lambda d: jax.ShapeDtypeStruct((batch, A_HEADS, d, seq // d, HEAD_DIM), BF16)
    return pl.pallas_call(
        _qkv_kernel,
        grid=(rows // ROW_TILE,),
        in_specs=[row_spec(D_MODEL), _const_spec((1, D_MODEL)), _const_spec((D_MODEL, QKV_WIDTH)),
                  row_spec(LANES), row_spec(LANES), row_spec(LANES)],
        out_specs=[out_spec(d) for d in dils],
        out_shape=[out_shape(d) for d in dils],
        scratch_shapes=[pltpu.VMEM((ROW_TILE, D_MODEL), BF16),
                        pltpu.VMEM((COL_CHUNK // HEAD_DIM, ROW_TILE, HEAD_DIM), F32)],
        compiler_params=_params("parallel"),
        name="even_qkv",
    )(x2, g.reshape(1, D_MODEL), w_qkv, *tabs)


def _mixer_kernel(q0, k0, v0, q1, k1, v1, q2, k2, v2, out_ref,
                  ck0, cv0, ck1, cv1, ck2, cv2, o_acc, l_acc):
    tile = pl.program_id(2)
    carries = (ck0, cv0, ck1, cv1, ck2, cv2)

    @pl.when(tile == 0)
    def _():
        for c in carries:
            c[...] = jnp.zeros_like(c)

    no_prev = jnp.where(tile > 0, 0.0, -jnp.inf).astype(F32)
    row = lax.broadcasted_iota(jnp.int32, (SPAN, SPAN), 0)
    col = lax.broadcasted_iota(jnp.int32, (SPAN, SPAN), 1)
    tri_prev = col >= row
    tri_cur = col <= row
    ones = jnp.ones((2 * SPAN, LANES), BF16)

    def attend(q, kk, vv, first):
        s = _dot_nt(q, kk) * SCALE_LOG2
        sp, sc = s[:, :SPAN], s[:, SPAN:]
        if first:
            sp = sp + no_prev
        sp = jnp.where(tri_prev, sp, -jnp.inf)
        sc = jnp.where(tri_cur, sc, -jnp.inf)
        m = jnp.max(jnp.maximum(sp, sc), axis=-1, keepdims=True)
        p = jnp.concatenate([jnp.exp2(sp - m), jnp.exp2(sc - m)], axis=1).astype(BF16)
        both = _dot(p, jnp.concatenate([vv, ones], axis=1))
        num, den = both[:, :HEAD_DIM], both[:, HEAD_DIM:]
        return num * (1.0 / den), m + jnp.log2(den)

    cat = lambda a, b: jnp.concatenate([a, b], axis=0)

    o, l = attend(q0[0, 0, 0, 0:SPAN, :], cat(ck0[...], k0[0, 0, 0, 0:SPAN, :]),
                  cat(cv0[...], v0[0, 0, 0, 0:SPAN, :]), True)
    o_acc[0, 0:SPAN, :] = o
    l_acc[0, 0:SPAN, :] = l
    def block0(jb):
        r0 = jb * SPAN if isinstance(jb, int) else pl.multiple_of(jb * SPAN, SPAN)
        both = pl.ds(r0 - SPAN, 2 * SPAN)
        o, l = attend(q0[0, 0, 0, pl.ds(r0, SPAN), :], k0[0, 0, 0, both, :], v0[0, 0, 0, both, :], False)
        o_acc[0, pl.ds(r0, SPAN), :] = o
        l_acc[0, pl.ds(r0, SPAN), :] = l

    _unrolled_loop(1, SUPER // SPAN, UNROLL_D1, block0)
    ck0[...] = k0[0, 0, 0, SUPER - SPAN:SUPER, :]
    cv0[...] = v0[0, 0, 0, SUPER - SPAN:SUPER, :]

    dil1 = DIL_GROUPS[1][1]
    n_blk1 = SUPER // (dil1 * SPAN)

    def residue1(rho):
        for jb in range(n_blk1):
            rows = slice(jb * SPAN, (jb + 1) * SPAN)
            if jb == 0:
                kk, vv = cat(ck1[rho], k1[0, 0, rho, rows, :]), cat(cv1[rho], v1[0, 0, rho, rows, :])
            else:
                both = slice((jb - 1) * SPAN, (jb + 1) * SPAN)
                kk, vv = k1[0, 0, rho, both, :], v1[0, 0, rho, both, :]
            o, l = attend(q1[0, 0, rho, rows, :], kk, vv, jb == 0)
            dst = pl.ds(jb * dil1 * SPAN + rho, SPAN, stride=dil1)
            o_acc[1, dst, :] = o
            l_acc[1, dst, :] = l
        last = slice((n_blk1 - 1) * SPAN, n_blk1 * SPAN)
        ck1[rho] = k1[0, 0, rho, last, :]
        cv1[rho] = v1[0, 0, rho, last, :]

    _unrolled_loop(0, dil1, UNROLL_D4, residue1)

    dil2 = DIL_GROUPS[2][1]

    def residue2(rho):
        o, l = attend(q2[0, 0, rho], cat(ck2[rho], k2[0, 0, rho]), cat(cv2[rho], v2[0, 0, rho]), True)
        dst = pl.ds(rho, SPAN, stride=dil2)
        o_acc[2, dst, :] = o
        l_acc[2, dst, :] = l
        ck2[rho] = k2[0, 0, rho]
        cv2[rho] = v2[0, 0, rho]

    _unrolled_loop(0, dil2, UNROLL_D16, residue2)

    def mix(i, carry):
        rows = pl.ds(pl.multiple_of(i * MIX_BAND, MIX_BAND), MIX_BAND)
        ls = [l_acc[gi, rows, :] for gi in range(N_GROUPS)]
        lmax = jnp.maximum(jnp.maximum(ls[0], ls[1]), ls[2])
        es = [jnp.exp2(l - lmax) for l in ls]
        acc = es[0] * o_acc[0, rows, :] + es[1] * o_acc[1, rows, :] + es[2] * o_acc[2, rows, :]
        out_ref[0, 0, rows, :] = (acc * (1.0 / (es[0] + es[1] + es[2]))).astype(BF16)
        return carry

    lax.fori_loop(0, SUPER // MIX_BAND, mix, 0)


def _dilated_mixer(pieces, batch, seq):
    ins, specs, carries = [], [], []
    for gi, (_, dil) in enumerate(DIL_GROUPS):
        spec = pl.BlockSpec((1, 1, dil, SUPER // dil, HEAD_DIM), lambda b, h, t: (b, h, 0, t, 0))
        ins += [pieces[2 * gi], pieces[2 * gi + 1], pieces[2 * N_GROUPS + gi]]
        specs += [spec] * 3
        shape = (SPAN, HEAD_DIM) if dil == 1 else (dil, SPAN, HEAD_DIM)
        carries += [pltpu.VMEM(shape, BF16)] * 2
    return pl.pallas_call(
        _mixer_kernel,
        grid=(batch, A_HEADS, seq // SUPER),
        in_specs=specs,
        out_specs=pl.BlockSpec((1, 1, SUPER, HEAD_DIM), lambda b, h, t: (b, h, t, 0)),
        out_shape=jax.ShapeDtypeStruct((batch, A_HEADS, seq, HEAD_DIM), BF16),
        scratch_shapes=carries + [pltpu.VMEM((N_GROUPS, SUPER, HEAD_DIM), F32)] * 2,
        compiler_params=_params("parallel", "parallel", "arbitrary"),
        name="dilated_mixer",
    )(*ins)


def _even_tail_kernel(x_ref, halo_ref, g_ref, w_ref, wpool_ref, pscale_ref, mk_ref, mv_ref,
                      wout_ref, a_ref, out_ref, h_ref, p_ref, y_ref, *, tiles_per_seq):
    seq_tile = pl.program_id(0) % tiles_per_seq
    _rmsnorm_rows(x_ref, g_ref, h_ref)
    _project(h_ref, w_ref, p_ref)
    z0 = B_WIDTH + M_WIDTH

    h_halo = _rmsnorm_bf16(halo_ref[...], g_ref[...])
    xb_halo = _dot(h_halo, w_ref[:, 0:B_WIDTH]) * (seq_tile > 0).astype(F32)
    run = jnp.concatenate([xb_halo, p_ref[:, 0:B_WIDTH]], axis=0)
    t = seq_tile * ROW_TILE + lax.broadcasted_iota(jnp.int32, (ROW_TILE, POOL_CH), 0)
    shift = 1
    for gi, w in enumerate(POOL_SIZES):
        while shift < w:
            run = run + pltpu.roll(run, shift, 0)
            shift *= 2
        lanes = slice(gi * POOL_CH, (gi + 1) * POOL_CH)
        cnt = jnp.minimum(t + 1, w).astype(F32)
        pooled = run[POOL_HALO:, lanes] / cnt - p_ref[:, lanes]
        yb = _dot(pooled.astype(BF16), wpool_ref[gi]) * pscale_ref[:, lanes]
        zb = p_ref[:, z0 + A_WIDTH + gi * POOL_CH:z0 + A_WIDTH + (gi + 1) * POOL_CH]
        y_ref[:, A_WIDTH + gi * POOL_CH:A_WIDTH + (gi + 1) * POOL_CH] = (yb * _silu(zb)).astype(BF16)

    m_out = _memory_attention(p_ref, B_WIDTH, mk_ref.at[0, 0], mv_ref.at[0, 0])
    for hd in range(MEM_HEADS):
        c0 = z0 + A_WIDTH + B_WIDTH + hd * HEAD_DIM
        y0 = A_WIDTH + B_WIDTH + hd * HEAD_DIM
        y_ref[:, y0:y0 + HEAD_DIM] = (m_out[hd] * _silu(p_ref[:, c0:c0 + HEAD_DIM])).astype(BF16)

    for hd in range(A_HEADS):
        lanes = slice(hd * HEAD_DIM, (hd + 1) * HEAD_DIM)
        za = p_ref[:, z0 + hd * HEAD_DIM:z0 + (hd + 1) * HEAD_DIM]
        y_ref[:, lanes] = (a_ref[0, hd].astype(F32) * _silu(za)).astype(BF16)

    for c in range(0, D_MODEL, COL_CHUNK):
        cols = slice(c, c + COL_CHUNK)
        out_ref[:, cols] = x_ref[:, cols] + _dot(y_ref[...], wout_ref[:, cols])


def _even_tail(x2, g, w_rest, w_pool, pool_scale, memkv, layer, w_out, a_out, batch, seq):
    rows = x2.shape[0]
    tiles_per_seq = seq // ROW_TILE
    halo_blocks = ROW_TILE // POOL_HALO
    row_spec = lambda w: pl.BlockSpec((ROW_TILE, w), lambda i: (i, 0))
    halo_spec = pl.BlockSpec((POOL_HALO, D_MODEL), lambda i: (jnp.maximum(i * halo_blocks - 1, 0), 0))
    mk_spec = pl.BlockSpec((1, 1, N_MEM, M_WIDTH), lambda i: (layer, i // tiles_per_seq, 0, 0))
    mv_spec = pl.BlockSpec((1, 1, N_MEM, M_WIDTH), lambda i: (layer, i // tiles_per_seq, 0, 1))
    return pl.pallas_call(
        functools.partial(_even_tail_kernel, tiles_per_seq=tiles_per_seq),
        grid=(rows // ROW_TILE,),
        in_specs=[row_spec(D_MODEL), halo_spec, _const_spec((1, D_MODEL)),
                  _const_spec((D_MODEL, EVEN_REST)), _const_spec((len(POOL_SIZES), POOL_CH, POOL_CH)),
                  _const_spec((1, B_WIDTH)), mk_spec, mv_spec, _const_spec((EVEN_MIX, D_MODEL)),
                  pl.BlockSpec((1, A_HEADS, ROW_TILE, HEAD_DIM),
                               lambda i: (i // tiles_per_seq, 0, i % tiles_per_seq, 0))],
        out_specs=row_spec(D_MODEL),
        out_shape=jax.ShapeDtypeStruct((rows, D_MODEL), F32),
        scratch_shapes=[pltpu.VMEM((ROW_TILE, D_MODEL), BF16), pltpu.VMEM((ROW_TILE, EVEN_REST), F32),
                        pltpu.VMEM((ROW_TILE, EVEN_MIX), BF16)],
        compiler_params=_params("parallel"),
        name="even_tail",
    )(x2, x2, g.reshape(1, D_MODEL), w_rest, w_pool, pool_scale.reshape(1, B_WIDTH),
      memkv, memkv, w_out, a_out)


def _even_layer(x2, tabs, memkv, layer, g, w_in, w_pool, pool_scale, w_out, batch, seq):
    w_in = w_in.astype(BF16)
    pieces = _qkv_project(x2, g, w_in[:, :QKV_WIDTH], tabs, batch, seq)
    a_out = _dilated_mixer(pieces, batch, seq)
    return _even_tail(x2, g, w_in[:, QKV_WIDTH:], w_pool.astype(BF16), pool_scale, memkv, layer,
                      w_out.astype(BF16), a_out, batch, seq)


def _odd_kernel(x_ref, g_ref, w_ref, lng_ref, lnb_ref, ws_ref, bs_ref, mk_ref, mv_ref, wout_ref,
                fg_ref, out_ref, h_ref, p_ref, vn_ref, y_ref, *, final_norm):
    _rmsnorm_rows(x_ref, g_ref, h_ref)
    _project(h_ref, w_ref, p_ref)
    z0 = 2 * C_WIDTH + M_WIDTH

    for r in range(0, ROW_TILE, ROW_BAND):
        v = p_ref[r:r + ROW_BAND, C_WIDTH:2 * C_WIDTH]
        vc = v - jnp.mean(v, axis=-1, keepdims=True)
        var = jnp.mean(vc * vc, axis=-1, keepdims=True)
        vn_ref[r:r + ROW_BAND, :] = ((vc * lax.rsqrt(var + EPS)) * lng_ref[...] + lnb_ref[...]).astype(BF16)

    tri_r = lax.broadcasted_iota(jnp.int32, (CHUNK, CHUNK), 0)
    tri_c = lax.broadcasted_iota(jnp.int32, (CHUNK, CHUNK), 1)
    causal = tri_c <= tri_r
    for gi in range(C_GROUPS):
        lanes = slice(gi * C_CH, (gi + 1) * C_CH)
        ws = jnp.where(causal, ws_ref[gi], jnp.zeros((CHUNK, CHUNK), BF16))
        bias = bs_ref[:, gi:gi + 1]
        for n in range(ROW_TILE // CHUNK):
            rows = slice(n * CHUNK, (n + 1) * CHUNK)
            mixed = _dot(ws, vn_ref[rows, lanes]) + bias
            gate = _silu(p_ref[rows, z0 + gi * C_CH:z0 + (gi + 1) * C_CH])
            y_ref[rows, lanes] = (p_ref[rows, lanes] * mixed * gate).astype(BF16)

    m_out = _memory_attention(p_ref, 2 * C_WIDTH, mk_ref.at[0, 0], mv_ref.at[0, 0])
    for hd in range(MEM_HEADS):
        c0 = z0 + C_WIDTH + hd * HEAD_DIM
        y0 = C_WIDTH + hd * HEAD_DIM
        y_ref[:, y0:y0 + HEAD_DIM] = (m_out[hd] * _silu(p_ref[:, c0:c0 + HEAD_DIM])).astype(BF16)

    res_ref = p_ref if final_norm else out_ref
    for c in range(0, D_MODEL, COL_CHUNK):
        cols = slice(c, c + COL_CHUNK)
        res_ref[:, cols] = x_ref[:, cols] + _dot(y_ref[...], wout_ref[:, cols])
    if final_norm:
        for r in range(0, ROW_TILE, ROW_BAND):
            res = p_ref[r:r + ROW_BAND, 0:D_MODEL]
            ms = jnp.mean(res * res, axis=-1, keepdims=True)
            out_ref[r:r + ROW_BAND, :] = (res * lax.rsqrt(ms + EPS)) * fg_ref[...]


def _odd_layer(x2, memkv, layer, g, w_in, ln_g, ln_b, w_s, b_s, w_out, final_g, final_norm, seq):
    rows = x2.shape[0]
    tiles_per_seq = seq // ROW_TILE
    row_spec = pl.BlockSpec((ROW_TILE, D_MODEL), lambda i: (i, 0))
    mk_spec = pl.BlockSpec((1, 1, N_MEM, M_WIDTH), lambda i: (layer, i // tiles_per_seq, 0, 0))
    mv_spec = pl.BlockSpec((1, 1, N_MEM, M_WIDTH), lambda i: (layer, i // tiles_per_seq, 0, 1))
    return pl.pallas_call(
        functools.partial(_odd_kernel, final_norm=final_norm),
        grid=(rows // ROW_TILE,),
        in_specs=[row_spec, _const_spec((1, D_MODEL)), _const_spec((D_MODEL, ODD_IN)),
                  _const_spec((1, C_WIDTH)), _const_spec((1, C_WIDTH)),
                  _const_spec((C_GROUPS, CHUNK, CHUNK)), _const_spec((CHUNK, C_GROUPS)),
                  mk_spec, mv_spec, _const_spec((ODD_MIX, D_MODEL)), _const_spec((1, D_MODEL))],
        out_specs=row_spec,
        out_shape=jax.ShapeDtypeStruct((rows, D_MODEL), F32),
        scratch_shapes=[pltpu.VMEM((ROW_TILE, D_MODEL), BF16), pltpu.VMEM((ROW_TILE, ODD_IN), F32),
                        pltpu.VMEM((ROW_TILE, C_WIDTH), BF16), pltpu.VMEM((ROW_TILE, ODD_MIX), BF16)],
        compiler_params=_params("parallel"),
        name="odd_layer",
    )(x2, g.reshape(1, D_MODEL), w_in.astype(BF16), ln_g.reshape(1, C_WIDTH), ln_b.reshape(1, C_WIDTH),
      w_s.astype(BF16), b_s.T, memkv, memkv, w_out.astype(BF16), final_g.reshape(1, D_MODEL))


def kernel(x, mem, positions, g_mem, even_norm_g, even_w_in, even_w_pool, even_pool_scale,
           even_w_mem_kv, even_w_out, odd_norm_g, odd_w_in, odd_ln_g, odd_ln_b, odd_w_s,
           odd_b_s, odd_w_mem_kv, odd_w_out, final_norm_g):
    batch, seq, _ = x.shape
    depth = even_norm_g.shape[0] + odd_norm_g.shape[0]
    assert seq % (ROW_TILE) == 0 and seq % (max(d for _, d in DIL_GROUPS) * SPAN) == 0

    tabs = _rope_tables(positions)
    w_kv = jnp.stack([(even_w_mem_kv if l % 2 == 0 else odd_w_mem_kv)[l // 2] for l in range(depth)])
    memkv = _memory_kv(mem, g_mem, w_kv.astype(BF16))

    x2 = x.reshape(batch * seq, D_MODEL)
    for layer in range(depth):
        i = layer // 2
        if layer % 2 == 0:
            x2 = _even_layer(x2, tabs, memkv, layer, even_norm_g[i], even_w_in[i], even_w_pool[i],
                             even_pool_scale[i], even_w_out[i], batch, seq)
        else:
            x2 = _odd_layer(x2, memkv, layer, odd_norm_g[i], odd_w_in[i], odd_ln_g[i], odd_ln_b[i],
                            odd_w_s[i], odd_b_s[i], odd_w_out[i], final_norm_g,
                            layer == depth - 1, seq)
    return x2.reshape(batch, seq, D_MODEL)
```

```python
import functools

import jax
import jax.numpy as jnp
from jax import lax
from jax.experimental import pallas as pl
from jax.experimental.pallas import tpu as pltpu

D_MODEL = 1024
HEAD_DIM = 128
EPS = 1e-6
ROPE_THETA = 500000.0
ROT_DIM = HEAD_DIM // 4
ROT_HALF = ROT_DIM // 2
N_MEM = 256
MEM_HEADS = 4
DIL_GROUPS = ((128, 1), (512, 4), (2048, 16))
A_HEADS = 4
POOL_SIZES = (2, 4, 8, 16)
POOL_CH = 128
CHUNK = 128
C_GROUPS = 8
C_CH = 128

A_WIDTH = A_HEADS * HEAD_DIM
B_WIDTH = len(POOL_SIZES) * POOL_CH
C_WIDTH = C_GROUPS * C_CH
M_WIDTH = MEM_HEADS * HEAD_DIM
EVEN_MIX = A_WIDTH + B_WIDTH + M_WIDTH
ODD_MIX = C_WIDTH + M_WIDTH
N_GROUPS = len(DIL_GROUPS)
A_QK_WIDTH = 2 * N_GROUPS * A_WIDTH
QKV_WIDTH = A_QK_WIDTH + A_WIDTH
EVEN_IN = QKV_WIDTH + B_WIDTH + M_WIDTH + EVEN_MIX
EVEN_REST = EVEN_IN - QKV_WIDTH
ODD_IN = 2 * C_WIDTH + M_WIDTH + ODD_MIX
SPAN = 128
SCALE = HEAD_DIM ** -0.5
SCALE_LOG2 = SCALE * 1.4426950408889634
POOL_HALO = max(POOL_SIZES)
SUPER = max(d for _, d in DIL_GROUPS) * SPAN
MIX_BAND = 256
UNROLL_D1 = 15
UNROLL_D4 = 4
UNROLL_D16 = 16

LANES = 128
ROW_TILE = 512
ROW_BAND = 128
ROPE_BAND = 64
ROPE_APART = 64
DEINT = 4
COL_CHUNK = 512
VMEM_LIMIT = 56 * 1024 * 1024

F32 = jnp.float32
BF16 = jnp.bfloat16


def _const_spec(shape):
    nd = len(shape)
    return pl.BlockSpec(shape, lambda *_: (0,) * nd, pipeline_mode=pl.Buffered(1))


def _params(*sem):
    return pltpu.CompilerParams(dimension_semantics=sem, vmem_limit_bytes=VMEM_LIMIT)


def _rmsnorm_bf16(x, g):
    ms = jnp.mean(x * x, axis=-1, keepdims=True)
    return ((x * lax.rsqrt(ms + EPS)) * g).astype(BF16)


def _rmsnorm_rows(x_ref, g_ref, h_ref):
    g = g_ref[...]
    for r in range(0, x_ref.shape[0], ROW_BAND):
        h_ref[r:r + ROW_BAND, :] = _rmsnorm_bf16(x_ref[r:r + ROW_BAND, :], g)


def _project(h_ref, w_ref, p_ref, col0=0):
    for c in range(0, p_ref.shape[1], COL_CHUNK):
        p_ref[:, c:c + COL_CHUNK] = _dot(h_ref[...], w_ref[:, col0 + c:col0 + c + COL_CHUNK])


def _unrolled_loop(lo, hi, unroll, fn):
    n = hi - lo
    assert n % unroll == 0
    if unroll == n:
        for i in range(lo, hi):
            fn(i)
        return

    def trip(t, carry):
        for s in range(unroll):
            fn(lo + t * unroll + s)
        return carry

    lax.fori_loop(0, n // unroll, trip, 0)


def _dot(a, b):
    return jnp.dot(a, b, preferred_element_type=F32)


def _dot_nt(a, b):
    return lax.dot_general(a, b, (((1,), (1,)), ((), ())), preferred_element_type=F32)


def _silu(z):
    return z * (1.0 / (1.0 + jnp.exp(-z)))


def _rope_kernel(pos_ref, inv_ref, cos_ref, sin_ref, nsin_ref):
    ang = pos_ref[...].astype(F32) * inv_ref[...]
    sin = jnp.sin(ang)
    cos_ref[...] = jnp.cos(ang)
    sin_ref[...] = sin
    nsin_ref[...] = -sin


def _rope_tables(positions):
    n = positions.size
    per_row = LANES // ROT_HALF
    inv = ROPE_THETA ** (-jnp.arange(0, ROT_DIM, 2, dtype=F32) / ROT_DIM)
    pos_c = jnp.repeat(positions.reshape(n // per_row, per_row), ROT_HALF, axis=1)
    inv_c = jnp.tile(inv, per_row).reshape(1, LANES)
    rows = n // per_row
    cos_c, sin_c, nsin_c = pl.pallas_call(
        _rope_kernel,
        out_shape=(jax.ShapeDtypeStruct((rows, LANES), F32),) * 3,
        name="rope_tables",
    )(pos_c, inv_c)
    cos, sin, nsin = (t.reshape(n, ROT_HALF) for t in (cos_c, sin_c, nsin_c))
    ones = jnp.ones((n, ROPE_APART - ROT_HALF), F32)
    zeros = jnp.zeros((n, ROPE_APART - ROT_HALF), F32)
    c_tab = jnp.concatenate([cos, ones, cos, ones], axis=1)
    s_tab = jnp.concatenate([nsin, zeros, sin, zeros], axis=1)
    return c_tab, s_tab


def _rope_column_order():
    blocks = list(range(HEAD_DIM // ROT_HALF))
    a, b = 1, ROPE_APART // ROT_HALF
    blocks[a], blocks[b] = blocks[b], blocks[a]
    head = [blk * ROT_HALF + i for blk in blocks for i in range(ROT_HALF)]
    order = [hd * HEAD_DIM + c for hd in range(A_QK_WIDTH // HEAD_DIM) for c in head]
    return order + list(range(A_QK_WIDTH, QKV_WIDTH))


def _memkv_kernel(mem_ref, g_ref, w_ref, out_ref):
    for r in range(0, mem_ref.shape[0], N_MEM):
        mem_n = _rmsnorm_bf16(mem_ref[r:r + N_MEM, :], g_ref[...])
        out_ref[0, r:r + N_MEM, :] = _dot(mem_n, w_ref[0]).astype(BF16)


def _memory_kv(mem, g_mem, w_kv):
    n_layers = w_kv.shape[0]
    batch = mem.shape[0]
    rows = batch * N_MEM
    out = pl.pallas_call(
        _memkv_kernel,
        grid=(n_layers,),
        in_specs=[_const_spec((rows, D_MODEL)), _const_spec((1, D_MODEL)),
                  pl.BlockSpec((1, D_MODEL, 2 * M_WIDTH), lambda l: (l, 0, 0))],
        out_specs=pl.BlockSpec((1, rows, 2 * M_WIDTH), lambda l: (l, 0, 0)),
        out_shape=jax.ShapeDtypeStruct((n_layers, rows, 2 * M_WIDTH), BF16),
        compiler_params=_params("parallel"),
        name="memory_kv",
    )(mem.reshape(rows, D_MODEL), g_mem.reshape(1, D_MODEL), w_kv.astype(BF16))
    return out.reshape(n_layers, batch, N_MEM, 2 * M_WIDTH)


def _memory_attention(p_ref, col0, mk_ref, mv_ref):
    outs = []
    ones = jnp.ones((N_MEM, LANES), BF16)
    for h in range(MEM_HEADS):
        lanes = slice(h * HEAD_DIM, (h + 1) * HEAD_DIM)
        q = p_ref[:, col0 + h * HEAD_DIM:col0 + (h + 1) * HEAD_DIM].astype(BF16)
        sc = _dot_nt(q, mk_ref[:, lanes]) * SCALE_LOG2
        e = jnp.exp2(sc - jnp.max(sc, axis=-1, keepdims=True)).astype(BF16)
        both = _dot(e, jnp.concatenate([mv_ref[:, lanes], ones], axis=1))
        outs.append(both[:, :HEAD_DIM] * (1.0 / both[:, HEAD_DIM:]))
    return outs


def _qkv_kernel(x_ref, g_ref, w_ref, c_ref, s_ref, *refs):
    n_qk = 2 * N_GROUPS
    qk_refs, v_refs = refs[:n_qk], refs[n_qk:n_qk + N_GROUPS]
    h_ref, acc_ref, st_ref = refs[n_qk + N_GROUPS:]
    heads = COL_CHUNK // HEAD_DIM
    _rmsnorm_rows(x_ref, g_ref, h_ref)

    def by_residue(head, targets):
        for b in range(DEINT):
            part = acc_ref[head, pl.ds(b, ROW_TILE // DEINT, stride=DEINT), :]
            for ref, dil in targets:
                if dil == DEINT:
                    ref[0, head, b] = part.astype(BF16)
                else:
                    st_ref[head, b] = part
                    for a in range(DEINT):
                        ref[0, head, DEINT * a + b] = (
                            st_ref[head, b, pl.ds(a, ROW_TILE // dil, stride=DEINT), :].astype(BF16))

    for c in range(QKV_WIDTH // COL_CHUNK):
        res = _dot(h_ref[...], w_ref[:, c * COL_CHUNK:(c + 1) * COL_CHUNK])
        for hh in range(heads):
            acc_ref[hh] = res[:, hh * HEAD_DIM:(hh + 1) * HEAD_DIM]
        if c < n_qk:
            ref, dil = qk_refs[c], DIL_GROUPS[c // 2][1]
            for r in range(0, ROW_TILE, ROPE_BAND):
                rows = slice(r, r + ROPE_BAND)
                c_tab, s_tab = c_ref[rows, :], s_ref[rows, :]
                for hh in range(heads):
                    blk = acc_ref[hh, rows, :]
                    blk = blk * c_tab + pltpu.roll(blk, ROPE_APART, 1) * s_tab
                    if dil == 1:
                        ref[0, hh, 0, rows, :] = blk.astype(BF16)
                    else:
                        acc_ref[hh, rows, :] = blk
            if dil > 1:
                for hh in range(heads):
                    by_residue(hh, [(ref, dil)])
        else:
            for hh in range(heads):
                v_refs[0][0, hh, 0] = acc_ref[hh].astype(BF16)
                by_residue(hh, [(v_refs[gi], DIL_GROUPS[gi][1]) for gi in range(1, N_GROUPS)])


def _qkv_project(x2, g, w_qkv, tabs, batch, seq):
    rows = x2.shape[0]
    tiles_per_seq = seq // ROW_TILE
    row_spec = lambda w: pl.BlockSpec((ROW_TILE, w), lambda i: (i, 0))
    dils = [DIL_GROUPS[c // 2][1] for c in range(2 * N_GROUPS)] + [d for _, d in DIL_GROUPS]
    out_spec = lambda d: pl.BlockSpec((1, A_HEADS, d, ROW_TILE // d, HEAD_DIM),
                                      lambda i: (i // tiles_per_seq, 0, 0, i % tiles_per_seq, 0))
    out_shape = lambda d: jax.ShapeDtypeStruct((batch, A_HEADS, d, seq // d, HEAD_DIM), BF16)
    return pl.pallas_call(
        _qkv_kernel,
        grid=(rows // ROW_TILE,),
        in_specs=[row_spec(D_MODEL), _const_spec((1, D_MODEL)), _const_spec((D_MODEL, QKV_WIDTH)),
                  row_spec(LANES), row_spec(LANES)],
        out_specs=[out_spec(d) for d in dils],
        out_shape=[out_shape(d) for d in dils],
        scratch_shapes=[pltpu.VMEM((ROW_TILE, D_MODEL), BF16),
                        pltpu.VMEM((COL_CHUNK // HEAD_DIM, ROW_TILE, HEAD_DIM), F32),
                        pltpu.VMEM((COL_CHUNK // HEAD_DIM, DEINT, ROW_TILE // DEINT, HEAD_DIM), F32)],
        compiler_params=_params("parallel"),
        name="even_qkv",
    )(x2, g.reshape(1, D_MODEL), w_qkv, *tabs)


def _mixer_kernel(q0, k0, v0, q1, k1, v1, q2, k2, v2, out_ref,
                  ck0, cv0, ck1, cv1, ck2, cv2, o_acc, l_acc):
    tile = pl.program_id(2)
    carries = (ck0, cv0, ck1, cv1, ck2, cv2)

    @pl.when(tile == 0)
    def _():
        for c in carries:
            c[...] = jnp.zeros_like(c)

    no_prev = jnp.where(tile > 0, 0.0, -jnp.inf).astype(F32)
    row = lax.broadcasted_iota(jnp.int32, (SPAN, SPAN), 0)
    col = lax.broadcasted_iota(jnp.int32, (SPAN, SPAN), 1)
    tri_prev = col >= row
    tri_cur = col <= row
    ones = jnp.ones((2 * SPAN, LANES), BF16)

    def attend(q, kk, vv, first):
        s = _dot_nt(q, kk) * SCALE_LOG2
        sp, sc = s[:, :SPAN], s[:, SPAN:]
        if first:
            sp = sp + no_prev
        sp = jnp.where(tri_prev, sp, -jnp.inf)
        sc = jnp.where(tri_cur, sc, -jnp.inf)
        m = jnp.max(jnp.maximum(sp, sc), axis=-1, keepdims=True)
        p = jnp.concatenate([jnp.exp2(sp - m), jnp.exp2(sc - m)], axis=1).astype(BF16)
        both = _dot(p, jnp.concatenate([vv, ones], axis=1))
        num, den = both[:, :HEAD_DIM], both[:, HEAD_DIM:]
        return num * (1.0 / den), m + jnp.log2(den)

    cat = lambda a, b: jnp.concatenate([a, b], axis=0)

    o, l = attend(q0[0, 0, 0, 0:SPAN, :], cat(ck0[...], k0[0, 0, 0, 0:SPAN, :]),
                  cat(cv0[...], v0[0, 0, 0, 0:SPAN, :]), True)
    o_acc[0, 0:SPAN, :] = o
    l_acc[0, 0:SPAN, :] = l
    def block0(jb):
        r0 = jb * SPAN if isinstance(jb, int) else pl.multiple_of(jb * SPAN, SPAN)
        both = pl.ds(r0 - SPAN, 2 * SPAN)
        o, l = attend(q0[0, 0, 0, pl.ds(r0, SPAN), :], k0[0, 0, 0, both, :], v0[0, 0, 0, both, :], False)
        o_acc[0, pl.ds(r0, SPAN), :] = o
        l_acc[0, pl.ds(r0, SPAN), :] = l

    _unrolled_loop(1, SUPER // SPAN, UNROLL_D1, block0)
    ck0[...] = k0[0, 0, 0, SUPER - SPAN:SUPER, :]
    cv0[...] = v0[0, 0, 0, SUPER - SPAN:SUPER, :]

    dil1 = DIL_GROUPS[1][1]
    n_blk1 = SUPER // (dil1 * SPAN)

    def residue1(rho):
        for jb in range(n_blk1):
            rows = slice(jb * SPAN, (jb + 1) * SPAN)
            if jb == 0:
                kk, vv = cat(ck1[rho], k1[0, 0, rho, rows, :]), cat(cv1[rho], v1[0, 0, rho, rows, :])
            else:
                both = slice((jb - 1) * SPAN, (jb + 1) * SPAN)
                kk, vv = k1[0, 0, rho, both, :], v1[0, 0, rho, both, :]
            o, l = attend(q1[0, 0, rho, rows, :], kk, vv, jb == 0)
            dst = pl.ds(jb * dil1 * SPAN + rho, SPAN, stride=dil1)
            o_acc[1, dst, :] = o
            l_acc[1, dst, :] = l
        last = slice((n_blk1 - 1) * SPAN, n_blk1 * SPAN)
        ck1[rho] = k1[0, 0, rho, last, :]
        cv1[rho] = v1[0, 0, rho, last, :]

    _unrolled_loop(0, dil1, UNROLL_D4, residue1)

    dil2 = DIL_GROUPS[2][1]

    def residue2(rho):
        o, l = attend(q2[0, 0, rho], cat(ck2[rho], k2[0, 0, rho]), cat(cv2[rho], v2[0, 0, rho]), True)
        dst = pl.ds(rho, SPAN, stride=dil2)
        o_acc[2, dst, :] = o
        l_acc[2, dst, :] = l
        ck2[rho] = k2[0, 0, rho]
        cv2[rho] = v2[0, 0, rho]

    _unrolled_loop(0, dil2, UNROLL_D16, residue2)

    def mix(i, carry):
        rows = pl.ds(pl.multiple_of(i * MIX_BAND, MIX_BAND), MIX_BAND)
        ls = [l_acc[gi, rows, :] for gi in range(N_GROUPS)]
        lmax = jnp.maximum(jnp.maximum(ls[0], ls[1]), ls[2])
        es = [jnp.exp2(l - lmax) for l in ls]
        acc = es[0] * o_acc[0, rows, :] + es[1] * o_acc[1, rows, :] + es[2] * o_acc[2, rows, :]
        out_ref[0, 0, rows, :] = (acc * (1.0 / (es[0] + es[1] + es[2]))).astype(BF16)
        return carry

    lax.fori_loop(0, SUPER // MIX_BAND, mix, 0)


def _dilated_mixer(pieces, batch, seq):
    ins, specs, carries = [], [], []
    for gi, (_, dil) in enumerate(DIL_GROUPS):
        spec = pl.BlockSpec((1, 1, dil, SUPER // dil, HEAD_DIM), lambda b, h, t: (b, h, 0, t, 0))
        ins += [pieces[2 * gi], pieces[2 * gi + 1], pieces[2 * N_GROUPS + gi]]
        specs += [spec] * 3
        shape = (SPAN, HEAD_DIM) if dil == 1 else (dil, SPAN, HEAD_DIM)
        carries += [pltpu.VMEM(shape, BF16)] * 2
    return pl.pallas_call(
        _mixer_kernel,
        grid=(batch, A_HEADS, seq // SUPER),
        in_specs=specs,
        out_specs=pl.BlockSpec((1, 1, SUPER, HEAD_DIM), lambda b, h, t: (b, h, t, 0)),
        out_shape=jax.ShapeDtypeStruct((batch, A_HEADS, seq, HEAD_DIM), BF16),
        scratch_shapes=carries + [pltpu.VMEM((N_GROUPS, SUPER, HEAD_DIM), F32)] * 2,
        compiler_params=_params("parallel", "parallel", "arbitrary"),
        name="dilated_mixer",
    )(*ins)


def _even_tail_kernel(x_ref, halo_ref, g_ref, w_ref, wpool_ref, pscale_ref, mk_ref, mv_ref,
                      wout_ref, a_ref, out_ref, h_ref, p_ref, y_ref, *, tiles_per_seq):
    seq_tile = pl.program_id(0) % tiles_per_seq
    _rmsnorm_rows(x_ref, g_ref, h_ref)
    _project(h_ref, w_ref, p_ref)
    z0 = B_WIDTH + M_WIDTH

    h_halo = _rmsnorm_bf16(halo_ref[...], g_ref[...])
    xb_halo = _dot(h_halo, w_ref[:, 0:B_WIDTH]) * (seq_tile > 0).astype(F32)
    run = jnp.concatenate([xb_halo, p_ref[:, 0:B_WIDTH]], axis=0)
    t = seq_tile * ROW_TILE + lax.broadcasted_iota(jnp.int32, (ROW_TILE, POOL_CH), 0)
    shift = 1
    for gi, w in enumerate(POOL_SIZES):
        while shift < w:
            run = run + pltpu.roll(run, shift, 0)
            shift *= 2
        lanes = slice(gi * POOL_CH, (gi + 1) * POOL_CH)
        cnt = jnp.minimum(t + 1, w).astype(F32)
        pooled = run[POOL_HALO:, lanes] / cnt - p_ref[:, lanes]
        yb = _dot(pooled.astype(BF16), wpool_ref[gi]) * pscale_ref[:, lanes]
        zb = p_ref[:, z0 + A_WIDTH + gi * POOL_CH:z0 + A_WIDTH + (gi + 1) * POOL_CH]
        y_ref[:, A_WIDTH + gi * POOL_CH:A_WIDTH + (gi + 1) * POOL_CH] = (yb * _silu(zb)).astype(BF16)

    m_out = _memory_attention(p_ref, B_WIDTH, mk_ref.at[0, 0], mv_ref.at[0, 0])
    for hd in range(MEM_HEADS):
        c0 = z0 + A_WIDTH + B_WIDTH + hd * HEAD_DIM
        y0 = A_WIDTH + B_WIDTH + hd * HEAD_DIM
        y_ref[:, y0:y0 + HEAD_DIM] = (m_out[hd] * _silu(p_ref[:, c0:c0 + HEAD_DIM])).astype(BF16)

    for hd in range(A_HEADS):
        lanes = slice(hd * HEAD_DIM, (hd + 1) * HEAD_DIM)
        za = p_ref[:, z0 + hd * HEAD_DIM:z0 + (hd + 1) * HEAD_DIM]
        y_ref[:, lanes] = (a_ref[0, hd].astype(F32) * _silu(za)).astype(BF16)

    for c in range(0, D_MODEL, COL_CHUNK):
        cols = slice(c, c + COL_CHUNK)
        out_ref[:, cols] = x_ref[:, cols] + _dot(y_ref[...], wout_ref[:, cols])


def _even_tail(x2, g, w_rest, w_pool, pool_scale, memkv, layer, w_out, a_out, batch, seq):
    rows = x2.shape[0]
    tiles_per_seq = seq // ROW_TILE
    halo_blocks = ROW_TILE // POOL_HALO
    row_spec = lambda w: pl.BlockSpec((ROW_TILE, w), lambda i: (i, 0))
    halo_spec = pl.BlockSpec((POOL_HALO, D_MODEL), lambda i: (jnp.maximum(i * halo_blocks - 1, 0), 0))
    mk_spec = pl.BlockSpec((1, 1, N_MEM, M_WIDTH), lambda i: (layer, i // tiles_per_seq, 0, 0))
    mv_spec = pl.BlockSpec((1, 1, N_MEM, M_WIDTH), lambda i: (layer, i // tiles_per_seq, 0, 1))
    return pl.pallas_call(
        functools.partial(_even_tail_kernel, tiles_per_seq=tiles_per_seq),
        grid=(rows // ROW_TILE,),
        in_specs=[row_spec(D_MODEL), halo_spec, _const_spec((1, D_MODEL)),
                  _const_spec((D_MODEL, EVEN_REST)), _const_spec((len(POOL_SIZES), POOL_CH, POOL_CH)),
                  _const_spec((1, B_WIDTH)), mk_spec, mv_spec, _const_spec((EVEN_MIX, D_MODEL)),
                  pl.BlockSpec((1, A_HEADS, ROW_TILE, HEAD_DIM),
                               lambda i: (i // tiles_per_seq, 0, i % tiles_per_seq, 0))],
        out_specs=row_spec(D_MODEL),
        out_shape=jax.ShapeDtypeStruct((rows, D_MODEL), F32),
        scratch_shapes=[pltpu.VMEM((ROW_TILE, D_MODEL), BF16), pltpu.VMEM((ROW_TILE, EVEN_REST), F32),
                        pltpu.VMEM((ROW_TILE, EVEN_MIX), BF16)],
        compiler_params=_params("parallel"),
        name="even_tail",
    )(x2, x2, g.reshape(1, D_MODEL), w_rest, w_pool, pool_scale.reshape(1, B_WIDTH),
      memkv, memkv, w_out, a_out)


def _even_layer(x2, tabs, memkv, layer, g, w_in, w_pool, pool_scale, w_out, batch, seq):
    assert all(d in (1, DEINT, DEINT * DEINT) for _, d in DIL_GROUPS)
    w_qkv = jnp.take(w_in, jnp.asarray(_rope_column_order(), jnp.int32), axis=1).astype(BF16)
    pieces = _qkv_project(x2, g, w_qkv, tabs, batch, seq)
    a_out = _dilated_mixer(pieces, batch, seq)
    return _even_tail(x2, g, w_in[:, QKV_WIDTH:].astype(BF16), w_pool.astype(BF16), pool_scale, memkv,
                      layer, w_out.astype(BF16), a_out, batch, seq)


def _odd_kernel(x_ref, g_ref, w_ref, lng_ref, lnb_ref, ws_ref, bs_ref, mk_ref, mv_ref, wout_ref,
                fg_ref, out_ref, h_ref, p_ref, vn_ref, y_ref, *, final_norm):
    _rmsnorm_rows(x_ref, g_ref, h_ref)
    _project(h_ref, w_ref, p_ref)
    z0 = 2 * C_WIDTH + M_WIDTH

    for r in range(0, ROW_TILE, ROW_BAND):
        v = p_ref[r:r + ROW_BAND, C_WIDTH:2 * C_WIDTH]
        vc = v - jnp.mean(v, axis=-1, keepdims=True)
        var = jnp.mean(vc * vc, axis=-1, keepdims=True)
        vn_ref[r:r + ROW_BAND, :] = ((vc * lax.rsqrt(var + EPS)) * lng_ref[...] + lnb_ref[...]).astype(BF16)

    tri_r = lax.broadcasted_iota(jnp.int32, (CHUNK, CHUNK), 0)
    tri_c = lax.broadcasted_iota(jnp.int32, (CHUNK, CHUNK), 1)
    causal = tri_c <= tri_r
    for gi in range(C_GROUPS):
        lanes = slice(gi * C_CH, (gi + 1) * C_CH)
        ws = jnp.where(causal, ws_ref[gi], jnp.zeros((CHUNK, CHUNK), BF16))
        bias = bs_ref[:, gi:gi + 1]
        n_chunks = ROW_TILE // CHUNK
        vn = jnp.concatenate([vn_ref[n * CHUNK:(n + 1) * CHUNK, lanes] for n in range(n_chunks)], axis=1)
        mixed_all = _dot(ws, vn)
        for n in range(n_chunks):
            rows = slice(n * CHUNK, (n + 1) * CHUNK)
            mixed = mixed_all[:, n * C_CH:(n + 1) * C_CH] + bias
            gate = _silu(p_ref[rows, z0 + gi * C_CH:z0 + (gi + 1) * C_CH])
            y_ref[rows, lanes] = (p_ref[rows, lanes] * mixed * gate).astype(BF16)

    m_out = _memory_attention(p_ref, 2 * C_WIDTH, mk_ref.at[0, 0], mv_ref.at[0, 0])
    for hd in range(MEM_HEADS):
        c0 = z0 + C_WIDTH + hd * HEAD_DIM
        y0 = C_WIDTH + hd * HEAD_DIM
        y_ref[:, y0:y0 + HEAD_DIM] = (m_out[hd] * _silu(p_ref[:, c0:c0 + HEAD_DIM])).astype(BF16)

    res_ref = p_ref if final_norm else out_ref
    for c in range(0, D_MODEL, COL_CHUNK):
        cols = slice(c, c + COL_CHUNK)
        res_ref[:, cols] = x_ref[:, cols] + _dot(y_ref[...], wout_ref[:, cols])
    if final_norm:
        for r in range(0, ROW_TILE, ROW_BAND):
            res = p_ref[r:r + ROW_BAND, 0:D_MODEL]
            ms = jnp.mean(res * res, axis=-1, keepdims=True)
            out_ref[r:r + ROW_BAND, :] = (res * lax.rsqrt(ms + EPS)) * fg_ref[...]


def _odd_layer(x2, memkv, layer, g, w_in, ln_g, ln_b, w_s, b_s, w_out, final_g, final_norm, seq):
    rows = x2.shape[0]
    tiles_per_seq = seq // ROW_TILE
    row_spec = pl.BlockSpec((ROW_TILE, D_MODEL), lambda i: (i, 0))
    mk_spec = pl.BlockSpec((1, 1, N_MEM, M_WIDTH), lambda i: (layer, i // tiles_per_seq, 0, 0))
    mv_spec = pl.BlockSpec((1, 1, N_MEM, M_WIDTH), lambda i: (layer, i // tiles_per_seq, 0, 1))
    return pl.pallas_call(
        functools.partial(_odd_kernel, final_norm=final_norm),
        grid=(rows // ROW_TILE,),
        in_specs=[row_spec, _const_spec((1, D_MODEL)), _const_spec((D_MODEL, ODD_IN)),
                  _const_spec((1, C_WIDTH)), _const_spec((1, C_WIDTH)),
                  _const_spec((C_GROUPS, CHUNK, CHUNK)), _const_spec((CHUNK, C_GROUPS)),
                  mk_spec, mv_spec, _const_spec((ODD_MIX, D_MODEL)), _const_spec((1, D_MODEL))],
        out_specs=row_spec,
        out_shape=jax.ShapeDtypeStruct((rows, D_MODEL), F32),
        scratch_shapes=[pltpu.VMEM((ROW_TILE, D_MODEL), BF16), pltpu.VMEM((ROW_TILE, ODD_IN), F32),
                        pltpu.VMEM((ROW_TILE, C_WIDTH), BF16), pltpu.VMEM((ROW_TILE, ODD_MIX), BF16)],
        compiler_params=_params("parallel"),
        name="odd_layer",
    )(x2, g.reshape(1, D_MODEL), w_in.astype(BF16), ln_g.reshape(1, C_WIDTH), ln_b.reshape(1, C_WIDTH),
      w_s.astype(BF16), b_s.T, memkv, memkv, w_out.astype(BF16), final_g.reshape(1, D_MODEL))


def kernel(x, mem, positions, g_mem, even_norm_g, even_w_in, even_w_pool, even_pool_scale,
           even_w_mem_kv, even_w_out, odd_norm_g, odd_w_in, odd_ln_g, odd_ln_b, odd_w_s,
           odd_b_s, odd_w_mem_kv, odd_w_out, final_norm_g):
    batch, seq, _ = x.shape
    depth = even_norm_g.shape[0] + odd_norm_g.shape[0]
    assert seq % (ROW_TILE) == 0 and seq % (max(d for _, d in DIL_GROUPS) * SPAN) == 0

    tabs = _rope_tables(positions)
    memkv_even = _memory_kv(mem, g_mem, even_w_mem_kv)
    memkv_odd = _memory_kv(mem, g_mem, odd_w_mem_kv)

    x2 = x.reshape(batch * seq, D_MODEL)
    for layer in range(depth):
        i = layer // 2
        if layer % 2 == 0:
            x2 = _even_layer(x2, tabs, memkv_even, i, even_norm_g[i], even_w_in[i], even_w_pool[i],
                             even_pool_scale[i], even_w_out[i], batch, seq)
        else:
            x2 = _odd_layer(x2, memkv_odd, i, odd_norm_g[i], odd_w_in[i], odd_ln_g[i], odd_ln_b[i],
                            odd_w_s[i], odd_b_s[i], odd_w_out[i], final_norm_g,
                            layer == depth - 1, seq)
    return x2.reshape(batch, seq, D_MODEL)
```

```python
import functools

import jax
import jax.numpy as jnp
from jax import lax
from jax.experimental import pallas as pl
from jax.experimental.pallas import tpu as pltpu

D_MODEL = 1024
HEAD_DIM = 128
EPS = 1e-6
ROPE_THETA = 500000.0
ROT_DIM = HEAD_DIM // 4
ROT_HALF = ROT_DIM // 2
N_MEM = 256
MEM_HEADS = 4
DIL_GROUPS = ((128, 1), (512, 4), (2048, 16))
A_HEADS = 4
POOL_SIZES = (2, 4, 8, 16)
POOL_CH = 128
CHUNK = 128
C_GROUPS = 8
C_CH = 128

A_WIDTH = A_HEADS * HEAD_DIM
B_WIDTH = len(POOL_SIZES) * POOL_CH
C_WIDTH = C_GROUPS * C_CH
M_WIDTH = MEM_HEADS * HEAD_DIM
EVEN_MIX = A_WIDTH + B_WIDTH + M_WIDTH
ODD_MIX = C_WIDTH + M_WIDTH
N_GROUPS = len(DIL_GROUPS)
A_QK_WIDTH = 2 * N_GROUPS * A_WIDTH
QKV_WIDTH = A_QK_WIDTH + A_WIDTH
EVEN_IN = QKV_WIDTH + B_WIDTH + M_WIDTH + EVEN_MIX
EVEN_REST = EVEN_IN - QKV_WIDTH
ODD_IN = 2 * C_WIDTH + M_WIDTH + ODD_MIX
SPAN = 128
SCALE = HEAD_DIM ** -0.5
SCALE_LOG2 = SCALE * 1.4426950408889634
POOL_HALO = max(POOL_SIZES)
SUPER = max(d for _, d in DIL_GROUPS) * SPAN
MIX_BAND = 256
UNROLL_D1 = 15
UNROLL_D4 = 4
UNROLL_D16 = 16

LANES = 128
QKV_TILE = 1024
TAIL_TILE = 1024
ODD_TILE = 1024
ROW_BAND = 128
ROPE_BAND = 64
ROPE_APART = 64
DEINT = 4
COL_CHUNK = 512
VMEM_LIMIT = 56 * 1024 * 1024

F32 = jnp.float32
BF16 = jnp.bfloat16


def _const_spec(shape):
    nd = len(shape)
    return pl.BlockSpec(shape, lambda *_: (0,) * nd, pipeline_mode=pl.Buffered(1))


def _params(*sem):
    return pltpu.CompilerParams(dimension_semantics=sem, vmem_limit_bytes=VMEM_LIMIT)


def _rmsnorm_bf16(x, g):
    ms = jnp.mean(x * x, axis=-1, keepdims=True)
    return ((x * lax.rsqrt(ms + EPS)) * g).astype(BF16)


def _rmsnorm_rows(x_ref, g_ref, h_ref):
    g = g_ref[...]
    for r in range(0, x_ref.shape[0], ROW_BAND):
        h_ref[r:r + ROW_BAND, :] = _rmsnorm_bf16(x_ref[r:r + ROW_BAND, :], g)


def _project(h_ref, w_ref, p_ref, col0=0):
    for c in range(0, p_ref.shape[1], COL_CHUNK):
        p_ref[:, c:c + COL_CHUNK] = _dot(h_ref[...], w_ref[:, col0 + c:col0 + c + COL_CHUNK])


def _unrolled_loop(lo, hi, unroll, fn):
    n = hi - lo
    assert n % unroll == 0
    if unroll == n:
        for i in range(lo, hi):
            fn(i)
        return

    def trip(t, carry):
        for s in range(unroll):
            fn(lo + t * unroll + s)
        return carry

    lax.fori_loop(0, n // unroll, trip, 0)


def _dot(a, b):
    return jnp.dot(a, b, preferred_element_type=F32)


def _dot_nt(a, b):
    return lax.dot_general(a, b, (((1,), (1,)), ((), ())), preferred_element_type=F32)


def _silu(z):
    return z * (1.0 / (1.0 + jnp.exp(-z)))


ROPE_PACK = ROPE_APART // ROT_HALF
ROPE_ROWS = 256


def _rope_kernel(pos_ref, inv_ref, sign_ref, c_ref, s_ref):
    ang = pos_ref[...].astype(F32) * inv_ref[...]
    cos = jnp.cos(ang)
    sin = jnp.sin(ang) * sign_ref[...]
    lane = lax.broadcasted_iota(jnp.int32, cos.shape, 1)
    rotary = (lane % ROPE_APART) < ROT_HALF
    for j in range(ROPE_PACK):
        shift = (LANES - ROT_HALF * j) % LANES
        cj = cos if shift == 0 else pltpu.roll(cos, shift, 1)
        sj = sin if shift == 0 else pltpu.roll(sin, shift, 1)
        rows = pl.ds(j, ROPE_ROWS, stride=ROPE_PACK)
        c_ref[rows, :] = jnp.where(rotary, cj, 1.0)
        s_ref[rows, :] = jnp.where(rotary, sj, 0.0)


def _rope_tables(positions):
    n = positions.size
    inv = ROPE_THETA ** (-jnp.arange(0, ROT_DIM, 2, dtype=F32) / ROT_DIM)
    slots = jnp.repeat(positions.reshape(n // ROPE_PACK, ROPE_PACK), ROT_HALF, axis=1)
    pos_c = jnp.tile(slots, (1, LANES // ROPE_APART))
    inv_c = jnp.tile(inv, LANES // ROT_HALF).reshape(1, LANES)
    sign = jnp.where(jnp.arange(LANES) < ROPE_APART, -1.0, 1.0).astype(F32).reshape(1, LANES)
    lane_spec = _const_spec((1, LANES))
    out_spec = pl.BlockSpec((ROPE_ROWS * ROPE_PACK, LANES), lambda i: (i, 0))
    return pl.pallas_call(
        _rope_kernel,
        grid=(n // (ROPE_ROWS * ROPE_PACK),),
        in_specs=[pl.BlockSpec((ROPE_ROWS, LANES), lambda i: (i, 0)), lane_spec, lane_spec],
        out_specs=[out_spec, out_spec],
        out_shape=(jax.ShapeDtypeStruct((n, LANES), F32),) * 2,
        compiler_params=_params("parallel"),
        name="rope_tables",
    )(pos_c, inv_c, sign)


def _reorder_qk_columns(w_in):
    n_heads, n_blk = A_QK_WIDTH // HEAD_DIM, HEAD_DIM // ROT_HALF
    a, b = 1, ROPE_APART // ROT_HALF
    qk = w_in[:, :A_QK_WIDTH].reshape(D_MODEL, n_heads, n_blk, ROT_HALF)
    qk = jnp.concatenate([qk[:, :, :a], qk[:, :, b:b + 1], qk[:, :, a + 1:b], qk[:, :, a:a + 1],
                          qk[:, :, b + 1:]], axis=2)
    return jnp.concatenate([qk.reshape(D_MODEL, A_QK_WIDTH), w_in[:, A_QK_WIDTH:QKV_WIDTH]], axis=1)


def _memkv_kernel(mem_ref, g_ref, w_ref, out_ref):
    for r in range(0, mem_ref.shape[0], N_MEM):
        mem_n = _rmsnorm_bf16(mem_ref[r:r + N_MEM, :], g_ref[...])
        out_ref[0, r:r + N_MEM, :] = _dot(mem_n, w_ref[0]).astype(BF16)


def _memory_kv(mem, g_mem, w_kv):
    n_layers = w_kv.shape[0]
    batch = mem.shape[0]
    rows = batch * N_MEM
    out = pl.pallas_call(
        _memkv_kernel,
        grid=(n_layers,),
        in_specs=[_const_spec((rows, D_MODEL)), _const_spec((1, D_MODEL)),
                  pl.BlockSpec((1, D_MODEL, 2 * M_WIDTH), lambda l: (l, 0, 0))],
        out_specs=pl.BlockSpec((1, rows, 2 * M_WIDTH), lambda l: (l, 0, 0)),
        out_shape=jax.ShapeDtypeStruct((n_layers, rows, 2 * M_WIDTH), BF16),
        compiler_params=_params("parallel"),
        name="memory_kv",
    )(mem.reshape(rows, D_MODEL), g_mem.reshape(1, D_MODEL), w_kv.astype(BF16))
    return out.reshape(n_layers, batch, N_MEM, 2 * M_WIDTH)


def _memory_attention(p_ref, col0, mk_ref, mv_ref):
    outs = []
    ones = jnp.ones((N_MEM, LANES), BF16)
    for h in range(MEM_HEADS):
        lanes = slice(h * HEAD_DIM, (h + 1) * HEAD_DIM)
        q = p_ref[:, col0 + h * HEAD_DIM:col0 + (h + 1) * HEAD_DIM].astype(BF16)
        sc = _dot_nt(q, mk_ref[:, lanes]) * SCALE_LOG2
        e = jnp.exp2(sc - jnp.max(sc, axis=-1, keepdims=True)).astype(BF16)
        both = _dot(e, jnp.concatenate([mv_ref[:, lanes], ones], axis=1))
        outs.append(both[:, :HEAD_DIM] * (1.0 / both[:, HEAD_DIM:]))
    return outs


def _qkv_kernel(x_ref, g_ref, w_ref, c_ref, s_ref, *refs):
    n_qk = 2 * N_GROUPS
    qk_refs, v_refs = refs[:n_qk], refs[n_qk:n_qk + N_GROUPS]
    h_ref, acc_ref, st_ref = refs[n_qk + N_GROUPS:]
    heads = COL_CHUNK // HEAD_DIM
    tile_rows = x_ref.shape[0]
    _rmsnorm_rows(x_ref, g_ref, h_ref)

    def by_residue(head, targets):
        for b in range(DEINT):
            part = acc_ref[head, pl.ds(b, tile_rows // DEINT, stride=DEINT), :]
            for ref, dil in targets:
                if dil == DEINT:
                    ref[0, head, b] = part.astype(BF16)
                else:
                    st_ref[head, b] = part
                    for a in range(DEINT):
                        ref[0, head, DEINT * a + b] = (
                            st_ref[head, b, pl.ds(a, tile_rows // dil, stride=DEINT), :].astype(BF16))

    for c in range(QKV_WIDTH // COL_CHUNK):
        res = _dot(h_ref[...], w_ref[:, c * COL_CHUNK:(c + 1) * COL_CHUNK])
        for hh in range(heads):
            acc_ref[hh] = res[:, hh * HEAD_DIM:(hh + 1) * HEAD_DIM]
        if c < n_qk:
            ref, dil = qk_refs[c], DIL_GROUPS[c // 2][1]
            for r in range(0, tile_rows, ROPE_BAND):
                rows = slice(r, r + ROPE_BAND)
                c_tab, s_tab = c_ref[rows, :], s_ref[rows, :]
                for hh in range(heads):
                    blk = acc_ref[hh, rows, :]
                    blk = blk * c_tab + pltpu.roll(blk, ROPE_APART, 1) * s_tab
                    if dil == 1:
                        ref[0, hh, 0, rows, :] = blk.astype(BF16)
                    else:
                        acc_ref[hh, rows, :] = blk
            if dil > 1:
                for hh in range(heads):
                    by_residue(hh, [(ref, dil)])
        else:
            for hh in range(heads):
                v_refs[0][0, hh, 0] = acc_ref[hh].astype(BF16)
                by_residue(hh, [(v_refs[gi], DIL_GROUPS[gi][1]) for gi in range(1, N_GROUPS)])


def _qkv_project(x2, g, w_qkv, tabs, batch, seq):
    rows = x2.shape[0]
    tile = QKV_TILE
    tiles_per_seq = seq // tile
    row_spec = lambda w: pl.BlockSpec((tile, w), lambda i: (i, 0))
    dils = [DIL_GROUPS[c // 2][1] for c in range(2 * N_GROUPS)] + [d for _, d in DIL_GROUPS]
    out_spec = lambda d: pl.BlockSpec((1, A_HEADS, d, tile // d, HEAD_DIM),
                                      lambda i: (i // tiles_per_seq, 0, 0, i % tiles_per_seq, 0))
    out_shape = lambda d: jax.ShapeDtypeStruct((batch, A_HEADS, d, seq // d, HEAD_DIM), BF16)
    return pl.pallas_call(
        _qkv_kernel,
        grid=(rows // tile,),
        in_specs=[row_spec(D_MODEL), _const_spec((1, D_MODEL)), _const_spec((D_MODEL, QKV_WIDTH)),
                  row_spec(LANES), row_spec(LANES)],
        out_specs=[out_spec(d) for d in dils],
        out_shape=[out_shape(d) for d in dils],
        scratch_shapes=[pltpu.VMEM((tile, D_MODEL), BF16),
                        pltpu.VMEM((COL_CHUNK // HEAD_DIM, tile, HEAD_DIM), F32),
                        pltpu.VMEM((COL_CHUNK // HEAD_DIM, DEINT, tile // DEINT, HEAD_DIM), F32)],
        compiler_params=_params("parallel"),
        name="even_qkv",
    )(x2, g.reshape(1, D_MODEL), w_qkv, *tabs)


def _mixer_kernel(q0, k0, v0, q1, k1, v1, q2, k2, v2, out_ref,
                  ck0, cv0, ck1, cv1, ck2, cv2, o_acc, l_acc):
    tile = pl.program_id(2)
    carries = (ck0, cv0, ck1, cv1, ck2, cv2)

    @pl.when(tile == 0)
    def _():
        for c in carries:
            c[...] = jnp.zeros_like(c)

    no_prev = jnp.where(tile > 0, 0.0, -jnp.inf).astype(F32)
    row = lax.broadcasted_iota(jnp.int32, (SPAN, SPAN), 0)
    col = lax.broadcasted_iota(jnp.int32, (SPAN, SPAN), 1)
    tri_prev = col >= row
    tri_cur = col <= row
    ones = jnp.ones((2 * SPAN, LANES), BF16)

    def attend(q, kk, vv, first):
        s = _dot_nt(q, kk) * SCALE_LOG2
        sp, sc = s[:, :SPAN], s[:, SPAN:]
        if first:
            sp = sp + no_prev
        sp = jnp.where(tri_prev, sp, -jnp.inf)
        sc = jnp.where(tri_cur, sc, -jnp.inf)
        m = jnp.max(jnp.maximum(sp, sc), axis=-1, keepdims=True)
        p = jnp.concatenate([jnp.exp2(sp - m), jnp.exp2(sc - m)], axis=1).astype(BF16)
        both = _dot(p, jnp.concatenate([vv, ones], axis=1))
        num, den = both[:, :HEAD_DIM], both[:, HEAD_DIM:]
        return num * (1.0 / den), m + jnp.log2(den)

    cat = lambda a, b: jnp.concatenate([a, b], axis=0)

    o, l = attend(q0[0, 0, 0, 0:SPAN, :], cat(ck0[...], k0[0, 0, 0, 0:SPAN, :]),
                  cat(cv0[...], v0[0, 0, 0, 0:SPAN, :]), True)
    o_acc[0, 0:SPAN, :] = o
    l_acc[0, 0:SPAN, :] = l
    def block0(jb):
        r0 = jb * SPAN if isinstance(jb, int) else pl.multiple_of(jb * SPAN, SPAN)
        both = pl.ds(r0 - SPAN, 2 * SPAN)
        o, l = attend(q0[0, 0, 0, pl.ds(r0, SPAN), :], k0[0, 0, 0, both, :], v0[0, 0, 0, both, :], False)
        o_acc[0, pl.ds(r0, SPAN), :] = o
        l_acc[0, pl.ds(r0, SPAN), :] = l

    _unrolled_loop(1, SUPER // SPAN, UNROLL_D1, block0)
    ck0[...] = k0[0, 0, 0, SUPER - SPAN:SUPER, :]
    cv0[...] = v0[0, 0, 0, SUPER - SPAN:SUPER, :]

    dil1 = DIL_GROUPS[1][1]
    n_blk1 = SUPER // (dil1 * SPAN)

    def residue1(rho):
        for jb in range(n_blk1):
            rows = slice(jb * SPAN, (jb + 1) * SPAN)
            if jb == 0:
                kk, vv = cat(ck1[rho], k1[0, 0, rho, rows, :]), cat(cv1[rho], v1[0, 0, rho, rows, :])
            else:
                both = slice((jb - 1) * SPAN, (jb + 1) * SPAN)
                kk, vv = k1[0, 0, rho, both, :], v1[0, 0, rho, both, :]
            o, l = attend(q1[0, 0, rho, rows, :], kk, vv, jb == 0)
            dst = pl.ds(jb * dil1 * SPAN + rho, SPAN, stride=dil1)
            o_acc[1, dst, :] = o
            l_acc[1, dst, :] = l
        last = slice((n_blk1 - 1) * SPAN, n_blk1 * SPAN)
        ck1[rho] = k1[0, 0, rho, last, :]
        cv1[rho] = v1[0, 0, rho, last, :]

    _unrolled_loop(0, dil1, UNROLL_D4, residue1)

    dil2 = DIL_GROUPS[2][1]

    def residue2(rho):
        o, l = attend(q2[0, 0, rho], cat(ck2[rho], k2[0, 0, rho]), cat(cv2[rho], v2[0, 0, rho]), True)
        dst = pl.ds(rho, SPAN, stride=dil2)
        o_acc[2, dst, :] = o
        l_acc[2, dst, :] = l
        ck2[rho] = k2[0, 0, rho]
        cv2[rho] = v2[0, 0, rho]

    _unrolled_loop(0, dil2, UNROLL_D16, residue2)

    def mix(i, carry):
        rows = pl.ds(pl.multiple_of(i * MIX_BAND, MIX_BAND), MIX_BAND)
        ls = [l_acc[gi, rows, :] for gi in range(N_GROUPS)]
        lmax = jnp.maximum(jnp.maximum(ls[0], ls[1]), ls[2])
        es = [jnp.exp2(l - lmax) for l in ls]
        acc = es[0] * o_acc[0, rows, :] + es[1] * o_acc[1, rows, :] + es[2] * o_acc[2, rows, :]
        out_ref[0, 0, rows, :] = (acc * (1.0 / (es[0] + es[1] + es[2]))).astype(BF16)
        return carry

    lax.fori_loop(0, SUPER // MIX_BAND, mix, 0)


def _dilated_mixer(pieces, batch, seq):
    ins, specs, carries = [], [], []
    for gi, (_, dil) in enumerate(DIL_GROUPS):
        spec = pl.BlockSpec((1, 1, dil, SUPER // dil, HEAD_DIM), lambda b, h, t: (b, h, 0, t, 0))
        ins += [pieces[2 * gi], pieces[2 * gi + 1], pieces[2 * N_GROUPS + gi]]
        specs += [spec] * 3
        shape = (SPAN, HEAD_DIM) if dil == 1 else (dil, SPAN, HEAD_DIM)
        carries += [pltpu.VMEM(shape, BF16)] * 2
    return pl.pallas_call(
        _mixer_kernel,
        grid=(batch, A_HEADS, seq // SUPER),
        in_specs=specs,
        out_specs=pl.BlockSpec((1, 1, SUPER, HEAD_DIM), lambda b, h, t: (b, h, t, 0)),
        out_shape=jax.ShapeDtypeStruct((batch, A_HEADS, seq, HEAD_DIM), BF16),
        scratch_shapes=carries + [pltpu.VMEM((N_GROUPS, SUPER, HEAD_DIM), F32)] * 2,
        compiler_params=_params("parallel", "parallel", "arbitrary"),
        name="dilated_mixer",
    )(*ins)


def _even_tail_kernel(x_ref, halo_ref, g_ref, w_ref, wpool_ref, pscale_ref, mk_ref, mv_ref,
                      wout_ref, a_ref, out_ref, h_ref, p_ref, y_ref, *, tiles_per_seq):
    seq_tile = pl.program_id(0) % tiles_per_seq
    tile_rows = x_ref.shape[0]
    _rmsnorm_rows(x_ref, g_ref, h_ref)
    _project(h_ref, w_ref, p_ref)
    z0 = B_WIDTH + M_WIDTH

    def out_project(k0):
        yk = y_ref[:, k0:k0 + COL_CHUNK]
        for c in range(0, D_MODEL, COL_CHUNK):
            cols = slice(c, c + COL_CHUNK)
            base = x_ref[:, cols] if k0 == 0 else out_ref[:, cols]
            out_ref[:, cols] = base + _dot(yk, wout_ref[k0:k0 + COL_CHUNK, cols])

    for hd in range(A_HEADS):
        lanes = slice(hd * HEAD_DIM, (hd + 1) * HEAD_DIM)
        za = p_ref[:, z0 + hd * HEAD_DIM:z0 + (hd + 1) * HEAD_DIM]
        y_ref[:, lanes] = (a_ref[0, hd].astype(F32) * _silu(za)).astype(BF16)
    out_project(0)

    h_halo = _rmsnorm_bf16(halo_ref[...], g_ref[...])
    xb_halo = _dot(h_halo, w_ref[:, 0:B_WIDTH]) * (seq_tile > 0).astype(F32)
    run = jnp.concatenate([xb_halo, p_ref[:, 0:B_WIDTH]], axis=0)
    t = seq_tile * tile_rows + lax.broadcasted_iota(jnp.int32, (tile_rows, POOL_CH), 0)
    shift = 1
    for gi, w in enumerate(POOL_SIZES):
        while shift < w:
            run = run + pltpu.roll(run, shift, 0)
            shift *= 2
        lanes = slice(gi * POOL_CH, (gi + 1) * POOL_CH)
        cnt = jnp.minimum(t + 1, w).astype(F32)
        pooled = run[POOL_HALO:, lanes] / cnt - p_ref[:, lanes]
        yb = _dot(pooled.astype(BF16), wpool_ref[gi]) * pscale_ref[:, lanes]
        zb = p_ref[:, z0 + A_WIDTH + gi * POOL_CH:z0 + A_WIDTH + (gi + 1) * POOL_CH]
        y_ref[:, A_WIDTH + gi * POOL_CH:A_WIDTH + (gi + 1) * POOL_CH] = (yb * _silu(zb)).astype(BF16)
    out_project(A_WIDTH)

    m_out = _memory_attention(p_ref, B_WIDTH, mk_ref.at[0, 0], mv_ref.at[0, 0])
    for hd in range(MEM_HEADS):
        c0 = z0 + A_WIDTH + B_WIDTH + hd * HEAD_DIM
        y0 = A_WIDTH + B_WIDTH + hd * HEAD_DIM
        y_ref[:, y0:y0 + HEAD_DIM] = (m_out[hd] * _silu(p_ref[:, c0:c0 + HEAD_DIM])).astype(BF16)
    out_project(A_WIDTH + B_WIDTH)


def _even_tail(x2, g, w_rest, w_pool, pool_scale, memkv, layer, w_out, a_out, batch, seq):
    rows = x2.shape[0]
    tile_rows = TAIL_TILE
    tiles_per_seq = seq // tile_rows
    halo_blocks = tile_rows // POOL_HALO
    row_spec = lambda w: pl.BlockSpec((tile_rows, w), lambda i: (i, 0))
    halo_spec = pl.BlockSpec((POOL_HALO, D_MODEL), lambda i: (jnp.maximum(i * halo_blocks - 1, 0), 0))
    mk_spec = pl.BlockSpec((1, 1, N_MEM, M_WIDTH), lambda i: (layer, i // tiles_per_seq, 0, 0))
    mv_spec = pl.BlockSpec((1, 1, N_MEM, M_WIDTH), lambda i: (layer, i // tiles_per_seq, 0, 1))
    return pl.pallas_call(
        functools.partial(_even_tail_kernel, tiles_per_seq=tiles_per_seq),
        grid=(rows // tile_rows,),
        in_specs=[row_spec(D_MODEL), halo_spec, _const_spec((1, D_MODEL)),
                  _const_spec((D_MODEL, EVEN_REST)), _const_spec((len(POOL_SIZES), POOL_CH, POOL_CH)),
                  _const_spec((1, B_WIDTH)), mk_spec, mv_spec, _const_spec((EVEN_MIX, D_MODEL)),
                  pl.BlockSpec((1, A_HEADS, tile_rows, HEAD_DIM),
                               lambda i: (i // tiles_per_seq, 0, i % tiles_per_seq, 0))],
        out_specs=row_spec(D_MODEL),
        out_shape=jax.ShapeDtypeStruct((rows, D_MODEL), F32),
        scratch_shapes=[pltpu.VMEM((tile_rows, D_MODEL), BF16), pltpu.VMEM((tile_rows, EVEN_REST), F32),
                        pltpu.VMEM((tile_rows, EVEN_MIX), BF16)],
        compiler_params=_params("parallel"),
        name="even_tail",
    )(x2, x2, g.reshape(1, D_MODEL), w_rest, w_pool, pool_scale.reshape(1, B_WIDTH),
      memkv, memkv, w_out, a_out)


def _even_layer(x2, tabs, memkv, layer, g, w_in, w_pool, pool_scale, w_out, batch, seq):
    assert all(d in (1, DEINT, DEINT * DEINT) for _, d in DIL_GROUPS)
    w_qkv = _reorder_qk_columns(w_in).astype(BF16)
    pieces = _qkv_project(x2, g, w_qkv, tabs, batch, seq)
    a_out = _dilated_mixer(pieces, batch, seq)
    return _even_tail(x2, g, w_in[:, QKV_WIDTH:].astype(BF16), w_pool.astype(BF16), pool_scale, memkv,
                      layer, w_out.astype(BF16), a_out, batch, seq)


def _odd_kernel(x_ref, g_ref, w_ref, lng_ref, lnb_ref, ws_ref, bs_ref, mk_ref, mv_ref, wout_ref,
                fg_ref, out_ref, h_ref, p_ref, vn_ref, y_ref, *, final_norm):
    tile_rows = x_ref.shape[0]
    _rmsnorm_rows(x_ref, g_ref, h_ref)
    _project(h_ref, w_ref, p_ref)
    z0 = 2 * C_WIDTH + M_WIDTH

    for r in range(0, tile_rows, ROW_BAND):
        v = p_ref[r:r + ROW_BAND, C_WIDTH:2 * C_WIDTH]
        vc = v - jnp.mean(v, axis=-1, keepdims=True)
        var = jnp.mean(vc * vc, axis=-1, keepdims=True)
        vn_ref[r:r + ROW_BAND, :] = ((vc * lax.rsqrt(var + EPS)) * lng_ref[...] + lnb_ref[...]).astype(BF16)

    res_ref, res0 = (p_ref, C_WIDTH) if final_norm else (out_ref, 0)

    def out_project(k0):
        yk = y_ref[:, k0:k0 + COL_CHUNK]
        for c in range(0, D_MODEL, COL_CHUNK):
            dst = slice(res0 + c, res0 + c + COL_CHUNK)
            base = x_ref[:, c:c + COL_CHUNK] if k0 == 0 else res_ref[:, dst]
            res_ref[:, dst] = base + _dot(yk, wout_ref[k0:k0 + COL_CHUNK, c:c + COL_CHUNK])

    tri_r = lax.broadcasted_iota(jnp.int32, (CHUNK, CHUNK), 0)
    tri_c = lax.broadcasted_iota(jnp.int32, (CHUNK, CHUNK), 1)
    causal = tri_c <= tri_r
    for gi in range(C_GROUPS):
        lanes = slice(gi * C_CH, (gi + 1) * C_CH)
        ws = jnp.where(causal, ws_ref[gi], jnp.zeros((CHUNK, CHUNK), BF16))
        bias = bs_ref[:, gi:gi + 1]
        n_chunks = tile_rows // CHUNK
        vn = jnp.concatenate([vn_ref[n * CHUNK:(n + 1) * CHUNK, lanes] for n in range(n_chunks)], axis=1)
        mixed_all = _dot(ws, vn)
        for n in range(n_chunks):
            rows = slice(n * CHUNK, (n + 1) * CHUNK)
            mixed = mixed_all[:, n * C_CH:(n + 1) * C_CH] + bias
            gate = _silu(p_ref[rows, z0 + gi * C_CH:z0 + (gi + 1) * C_CH])
            y_ref[rows, lanes] = (p_ref[rows, lanes] * mixed * gate).astype(BF16)
        if (gi + 1) * C_CH % COL_CHUNK == 0:
            out_project((gi + 1) * C_CH - COL_CHUNK)

    m_out = _memory_attention(p_ref, 2 * C_WIDTH, mk_ref.at[0, 0], mv_ref.at[0, 0])
    for hd in range(MEM_HEADS):
        c0 = z0 + C_WIDTH + hd * HEAD_DIM
        y0 = C_WIDTH + hd * HEAD_DIM
        y_ref[:, y0:y0 + HEAD_DIM] = (m_out[hd] * _silu(p_ref[:, c0:c0 + HEAD_DIM])).astype(BF16)
    out_project(C_WIDTH)

    if final_norm:
        for r in range(0, tile_rows, ROW_BAND):
            res = p_ref[r:r + ROW_BAND, res0:res0 + D_MODEL]
            ms = jnp.mean(res * res, axis=-1, keepdims=True)
            out_ref[r:r + ROW_BAND, :] = (res * lax.rsqrt(ms + EPS)) * fg_ref[...]


def _odd_layer(x2, memkv, layer, g, w_in, ln_g, ln_b, w_s, b_s, w_out, final_g, final_norm, seq):
    rows = x2.shape[0]
    tile_rows = ODD_TILE
    tiles_per_seq = seq // tile_rows
    row_spec = pl.BlockSpec((tile_rows, D_MODEL), lambda i: (i, 0))
    mk_spec = pl.BlockSpec((1, 1, N_MEM, M_WIDTH), lambda i: (layer, i // tiles_per_seq, 0, 0))
    mv_spec = pl.BlockSpec((1, 1, N_MEM, M_WIDTH), lambda i: (layer, i // tiles_per_seq, 0, 1))
    return pl.pallas_call(
        functools.partial(_odd_kernel, final_norm=final_norm),
        grid=(rows // tile_rows,),
        in_specs=[row_spec, _const_spec((1, D_MODEL)), _const_spec((D_MODEL, ODD_IN)),
                  _const_spec((1, C_WIDTH)), _const_spec((1, C_WIDTH)),
                  _const_spec((C_GROUPS, CHUNK, CHUNK)), _const_spec((CHUNK, C_GROUPS)),
                  mk_spec, mv_spec, _const_spec((ODD_MIX, D_MODEL)), _const_spec((1, D_MODEL))],
        out_specs=row_spec,
        out_shape=jax.ShapeDtypeStruct((rows, D_MODEL), F32),
        scratch_shapes=[pltpu.VMEM((tile_rows, D_MODEL), BF16), pltpu.VMEM((tile_rows, ODD_IN), F32),
                        pltpu.VMEM((tile_rows, C_WIDTH), BF16), pltpu.VMEM((tile_rows, ODD_MIX), BF16)],
        compiler_params=_params("parallel"),
        name="odd_layer",
    )(x2, g.reshape(1, D_MODEL), w_in.astype(BF16), ln_g.reshape(1, C_WIDTH), ln_b.reshape(1, C_WIDTH),
      w_s.astype(BF16), b_s.T, memkv, memkv, w_out.astype(BF16), final_g.reshape(1, D_MODEL))


def kernel(x, mem, positions, g_mem, even_norm_g, even_w_in, even_w_pool, even_pool_scale,
           even_w_mem_kv, even_w_out, odd_norm_g, odd_w_in, odd_ln_g, odd_ln_b, odd_w_s,
           odd_b_s, odd_w_mem_kv, odd_w_out, final_norm_g):
    batch, seq, _ = x.shape
    depth = even_norm_g.shape[0] + odd_norm_g.shape[0]
    assert all(seq % t == 0 for t in (QKV_TILE, TAIL_TILE, ODD_TILE, SUPER))

    tabs = _rope_tables(positions)
    memkv_even = _memory_kv(mem, g_mem, even_w_mem_kv)
    memkv_odd = _memory_kv(mem, g_mem, odd_w_mem_kv)

    x2 = x.reshape(batch * seq, D_MODEL)
    for layer in range(depth):
        i = layer // 2
        if layer % 2 == 0:
            x2 = _even_layer(x2, tabs, memkv_even, i, even_norm_g[i], even_w_in[i], even_w_pool[i],
                             even_pool_scale[i], even_w_out[i], batch, seq)
        else:
            x2 = _odd_layer(x2, memkv_odd, i, odd_norm_g[i], odd_w_in[i], odd_ln_g[i], odd_ln_b[i],
                            odd_w_s[i], odd_b_s[i], odd_w_out[i], final_norm_g,
                            layer == depth - 1, seq)
    return x2.reshape(batch, seq, D_MODEL)
```

```python
import functools

import jax
import jax.numpy as jnp
from jax import lax
from jax.experimental import pallas as pl
from jax.experimental.pallas import tpu as pltpu

D_MODEL = 1024
HEAD_DIM = 128
EPS = 1e-6
ROPE_THETA = 500000.0
ROT_DIM = HEAD_DIM // 4
ROT_HALF = ROT_DIM // 2
N_MEM = 256
MEM_HEADS = 4
DIL_GROUPS = ((128, 1), (512, 4), (2048, 16))
A_HEADS = 4
POOL_SIZES = (2, 4, 8, 16)
POOL_CH = 128
CHUNK = 128
C_GROUPS = 8
C_CH = 128

A_WIDTH = A_HEADS * HEAD_DIM
B_WIDTH = len(POOL_SIZES) * POOL_CH
C_WIDTH = C_GROUPS * C_CH
M_WIDTH = MEM_HEADS * HEAD_DIM
EVEN_MIX = A_WIDTH + B_WIDTH + M_WIDTH
ODD_MIX = C_WIDTH + M_WIDTH
N_GROUPS = len(DIL_GROUPS)
A_QK_WIDTH = 2 * N_GROUPS * A_WIDTH
QKV_WIDTH = A_QK_WIDTH + A_WIDTH
EVEN_IN = QKV_WIDTH + B_WIDTH + M_WIDTH + EVEN_MIX
EVEN_REST = EVEN_IN - QKV_WIDTH
ODD_IN = 2 * C_WIDTH + M_WIDTH + ODD_MIX
SPAN = 128
SCALE = HEAD_DIM ** -0.5
SCALE_LOG2 = SCALE * 1.4426950408889634
POOL_HALO = max(POOL_SIZES)
SUPER = max(d for _, d in DIL_GROUPS) * SPAN
MIX_BAND = 256
UNROLL_D1 = 15
UNROLL_D4 = 4
UNROLL_D16 = 16

LANES = 128
QKV_TILE = 1024
TAIL_TILE = 1024
ODD_TILE = 1024
ROW_BAND = 128
ROPE_BAND = 64
ROPE_APART = 64
DEINT = 4
WEIGHT_ROWS = 256
COL_CHUNK = 512
VMEM_LIMIT = 56 * 1024 * 1024

F32 = jnp.float32
BF16 = jnp.bfloat16


def _const_spec(shape):
    nd = len(shape)
    return pl.BlockSpec(shape, lambda *_: (0,) * nd, pipeline_mode=pl.Buffered(1))


def _layer_spec(shape, layer):
    nd = len(shape)
    return pl.BlockSpec((None,) + tuple(shape), lambda *_: (layer,) + (0,) * nd,
                        pipeline_mode=pl.Buffered(1))


def _params(*sem):
    return pltpu.CompilerParams(dimension_semantics=sem, vmem_limit_bytes=VMEM_LIMIT)


def _rmsnorm_bf16(x, g):
    ms = jnp.mean(x * x, axis=-1, keepdims=True)
    return ((x * lax.rsqrt(ms + EPS)) * g).astype(BF16)


def _rmsnorm_rows(x_ref, g_ref, h_ref):
    g = g_ref[...]
    for r in range(0, x_ref.shape[0], ROW_BAND):
        h_ref[r:r + ROW_BAND, :] = _rmsnorm_bf16(x_ref[r:r + ROW_BAND, :], g)


def _project(h_ref, w_ref, p_ref, col0=0):
    for c in range(0, p_ref.shape[1], COL_CHUNK):
        p_ref[:, c:c + COL_CHUNK] = _dot(h_ref[...], w_ref[:, col0 + c:col0 + c + COL_CHUNK])


def _unrolled_loop(lo, hi, unroll, fn):
    n = hi - lo
    assert n % unroll == 0
    if unroll == n:
        for i in range(lo, hi):
            fn(i)
        return

    def trip(t, carry):
        for s in range(unroll):
            fn(lo + t * unroll + s)
        return carry

    lax.fori_loop(0, n // unroll, trip, 0)


def _dot(a, b):
    return jnp.dot(a, b, preferred_element_type=F32)


def _dot_nt(a, b):
    return lax.dot_general(a, b, (((1,), (1,)), ((), ())), preferred_element_type=F32)


def _silu(z):
    return z * (1.0 / (1.0 + jnp.exp(-z)))


ROPE_PACK = ROPE_APART // ROT_HALF
ROPE_ROWS = 256


def _rope_kernel(pos_ref, inv_ref, sign_ref, c_ref, s_ref):
    ang = pos_ref[...].astype(F32) * inv_ref[...]
    cos = jnp.cos(ang)
    sin = jnp.sin(ang) * sign_ref[...]
    lane = lax.broadcasted_iota(jnp.int32, cos.shape, 1)
    rotary = (lane % ROPE_APART) < ROT_HALF
    for j in range(ROPE_PACK):
        shift = (LANES - ROT_HALF * j) % LANES
        cj = cos if shift == 0 else pltpu.roll(cos, shift, 1)
        sj = sin if shift == 0 else pltpu.roll(sin, shift, 1)
        rows = pl.ds(j, ROPE_ROWS, stride=ROPE_PACK)
        c_ref[rows, :] = jnp.where(rotary, cj, 1.0)
        s_ref[rows, :] = jnp.where(rotary, sj, 0.0)


def _rope_tables(positions):
    n = positions.size
    inv = ROPE_THETA ** (-jnp.arange(0, ROT_DIM, 2, dtype=F32) / ROT_DIM)
    slots = jnp.repeat(positions.reshape(n // ROPE_PACK, ROPE_PACK), ROT_HALF, axis=1)
    pos_c = jnp.tile(slots, (1, LANES // ROPE_APART))
    inv_c = jnp.tile(inv, LANES // ROT_HALF).reshape(1, LANES)
    sign = jnp.where(jnp.arange(LANES) < ROPE_APART, -1.0, 1.0).astype(F32).reshape(1, LANES)
    lane_spec = _const_spec((1, LANES))
    out_spec = pl.BlockSpec((ROPE_ROWS * ROPE_PACK, LANES), lambda i: (i, 0))
    return pl.pallas_call(
        _rope_kernel,
        grid=(n // (ROPE_ROWS * ROPE_PACK),),
        in_specs=[pl.BlockSpec((ROPE_ROWS, LANES), lambda i: (i, 0)), lane_spec, lane_spec],
        out_specs=[out_spec, out_spec],
        out_shape=(jax.ShapeDtypeStruct((n, LANES), F32),) * 2,
        compiler_params=_params("parallel"),
        name="rope_tables",
    )(pos_c, inv_c, sign)


def _even_weights_kernel(w_ref, qkv_ref, rest_ref):
    lane = lax.broadcasted_iota(jnp.int32, (w_ref.shape[0], HEAD_DIM), 1)
    low = (lane >= ROT_HALF) & (lane < ROT_DIM)
    high = (lane >= ROPE_APART) & (lane < ROPE_APART + ROT_HALF)
    gap = ROPE_APART - ROT_HALF
    for hd in range(A_QK_WIDTH // HEAD_DIM):
        lanes = slice(hd * HEAD_DIM, (hd + 1) * HEAD_DIM)
        w = w_ref[:, lanes]
        w = jnp.where(low, pltpu.roll(w, HEAD_DIM - gap, 1), jnp.where(high, pltpu.roll(w, gap, 1), w))
        qkv_ref[:, lanes] = w.astype(BF16)
    qkv_ref[:, A_QK_WIDTH:] = w_ref[:, A_QK_WIDTH:QKV_WIDTH].astype(BF16)
    rest_ref[...] = w_ref[:, QKV_WIDTH:].astype(BF16)


def _even_weights(w_in):
    n_layers = w_in.shape[0]
    spec = lambda w: pl.BlockSpec((None, WEIGHT_ROWS, w), lambda l, r: (l, r, 0))
    return pl.pallas_call(
        _even_weights_kernel,
        grid=(n_layers, D_MODEL // WEIGHT_ROWS),
        in_specs=[spec(EVEN_IN)],
        out_specs=[spec(QKV_WIDTH), spec(EVEN_REST)],
        out_shape=[jax.ShapeDtypeStruct((n_layers, D_MODEL, QKV_WIDTH), BF16),
                   jax.ShapeDtypeStruct((n_layers, D_MODEL, EVEN_REST), BF16)],
        compiler_params=_params("parallel", "parallel"),
        name="even_weights",
    )(w_in)


def _memkv_kernel(mem_ref, g_ref, w_ref, out_ref):
    for r in range(0, mem_ref.shape[0], N_MEM):
        mem_n = _rmsnorm_bf16(mem_ref[r:r + N_MEM, :], g_ref[...])
        out_ref[0, r:r + N_MEM, :] = _dot(mem_n, w_ref[0]).astype(BF16)


def _memory_kv(mem, g_mem, w_kv):
    n_layers = w_kv.shape[0]
    batch = mem.shape[0]
    rows = batch * N_MEM
    out = pl.pallas_call(
        _memkv_kernel,
        grid=(n_layers,),
        in_specs=[_const_spec((rows, D_MODEL)), _const_spec((1, D_MODEL)),
                  pl.BlockSpec((1, D_MODEL, 2 * M_WIDTH), lambda l: (l, 0, 0))],
        out_specs=pl.BlockSpec((1, rows, 2 * M_WIDTH), lambda l: (l, 0, 0)),
        out_shape=jax.ShapeDtypeStruct((n_layers, rows, 2 * M_WIDTH), BF16),
        compiler_params=_params("parallel"),
        name="memory_kv",
    )(mem.reshape(rows, D_MODEL), g_mem.reshape(1, D_MODEL), w_kv.astype(BF16))
    return out.reshape(n_layers, batch, N_MEM, 2 * M_WIDTH)


def _memory_attention(p_ref, col0, mk_ref, mv_ref):
    outs = []
    ones = jnp.ones((N_MEM, LANES), BF16)
    for h in range(MEM_HEADS):
        lanes = slice(h * HEAD_DIM, (h + 1) * HEAD_DIM)
        q = p_ref[:, col0 + h * HEAD_DIM:col0 + (h + 1) * HEAD_DIM].astype(BF16)
        sc = _dot_nt(q, mk_ref[:, lanes]) * SCALE_LOG2
        e = jnp.exp2(sc - jnp.max(sc, axis=-1, keepdims=True)).astype(BF16)
        both = _dot(e, jnp.concatenate([mv_ref[:, lanes], ones], axis=1))
        outs.append(both[:, :HEAD_DIM] * (1.0 / both[:, HEAD_DIM:]))
    return outs


def _qkv_kernel(x_ref, g_ref, w_ref, c_ref, s_ref, *refs):
    n_qk = 2 * N_GROUPS
    qk_refs, v_refs = refs[:n_qk], refs[n_qk:n_qk + N_GROUPS]
    h_ref, acc_ref, st_ref = refs[n_qk + N_GROUPS:]
    heads = COL_CHUNK // HEAD_DIM
    tile_rows = x_ref.shape[0]
    _rmsnorm_rows(x_ref, g_ref, h_ref)

    def by_residue(head, targets):
        for b in range(DEINT):
            part = acc_ref[head, pl.ds(b, tile_rows // DEINT, stride=DEINT), :]
            for ref, dil in targets:
                if dil == DEINT:
                    ref[0, head, b] = part.astype(BF16)
                else:
                    st_ref[head, b] = part
                    for a in range(DEINT):
                        ref[0, head, DEINT * a + b] = (
                            st_ref[head, b, pl.ds(a, tile_rows // dil, stride=DEINT), :].astype(BF16))

    for c in range(QKV_WIDTH // COL_CHUNK):
        res = _dot(h_ref[...], w_ref[:, c * COL_CHUNK:(c + 1) * COL_CHUNK])
        for hh in range(heads):
            acc_ref[hh] = res[:, hh * HEAD_DIM:(hh + 1) * HEAD_DIM]
        if c < n_qk:
            ref, dil = qk_refs[c], DIL_GROUPS[c // 2][1]
            for r in range(0, tile_rows, ROPE_BAND):
                rows = slice(r, r + ROPE_BAND)
                c_tab, s_tab = c_ref[rows, :], s_ref[rows, :]
                for hh in range(heads):
                    blk = acc_ref[hh, rows, :]
                    blk = blk * c_tab + pltpu.roll(blk, ROPE_APART, 1) * s_tab
                    if dil == 1:
                        ref[0, hh, 0, rows, :] = blk.astype(BF16)
                    else:
                        acc_ref[hh, rows, :] = blk
            if dil > 1:
                for hh in range(heads):
                    by_residue(hh, [(ref, dil)])
        else:
            for hh in range(heads):
                v_refs[0][0, hh, 0] = acc_ref[hh].astype(BF16)
                by_residue(hh, [(v_refs[gi], DIL_GROUPS[gi][1]) for gi in range(1, N_GROUPS)])


def _qkv_project(x2, g, w_qkv, layer, tabs, batch, seq):
    rows = x2.shape[0]
    tile = QKV_TILE
    tiles_per_seq = seq // tile
    row_spec = lambda w: pl.BlockSpec((tile, w), lambda i: (i, 0))
    dils = [DIL_GROUPS[c // 2][1] for c in range(2 * N_GROUPS)] + [d for _, d in DIL_GROUPS]
    out_spec = lambda d: pl.BlockSpec((1, A_HEADS, d, tile // d, HEAD_DIM),
                                      lambda i: (i // tiles_per_seq, 0, 0, i % tiles_per_seq, 0))
    out_shape = lambda d: jax.ShapeDtypeStruct((batch, A_HEADS, d, seq // d, HEAD_DIM), BF16)
    return pl.pallas_call(
        _qkv_kernel,
        grid=(rows // tile,),
        in_specs=[row_spec(D_MODEL), _const_spec((1, D_MODEL)), _layer_spec((D_MODEL, QKV_WIDTH), layer),
                  row_spec(LANES), row_spec(LANES)],
        out_specs=[out_spec(d) for d in dils],
        out_shape=[out_shape(d) for d in dils],
        scratch_shapes=[pltpu.VMEM((tile, D_MODEL), BF16),
                        pltpu.VMEM((COL_CHUNK // HEAD_DIM, tile, HEAD_DIM), F32),
                        pltpu.VMEM((COL_CHUNK // HEAD_DIM, DEINT, tile // DEINT, HEAD_DIM), F32)],
        compiler_params=_params("parallel"),
        name="even_qkv",
    )(x2, g.reshape(1, D_MODEL), w_qkv, *tabs)


def _mixer_kernel(q0, k0, v0, q1, k1, v1, q2, k2, v2, out_ref,
                  ck0, cv0, ck1, cv1, ck2, cv2, o_acc, l_acc):
    tile = pl.program_id(2)
    carries = (ck0, cv0, ck1, cv1, ck2, cv2)

    @pl.when(tile == 0)
    def _():
        for c in carries:
            c[...] = jnp.zeros_like(c)

    no_prev = jnp.where(tile > 0, 0.0, -jnp.inf).astype(F32)
    row = lax.broadcasted_iota(jnp.int32, (SPAN, SPAN), 0)
    col = lax.broadcasted_iota(jnp.int32, (SPAN, SPAN), 1)
    tri_prev = col >= row
    tri_cur = col <= row
    ones = jnp.ones((2 * SPAN, LANES), BF16)

    def attend(q, kk, vv, first):
        s = _dot_nt(q, kk) * SCALE_LOG2
        sp, sc = s[:, :SPAN], s[:, SPAN:]
        if first:
            sp = sp + no_prev
        sp = jnp.where(tri_prev, sp, -jnp.inf)
        sc = jnp.where(tri_cur, sc, -jnp.inf)
        m = jnp.max(jnp.maximum(sp, sc), axis=-1, keepdims=True)
        p = jnp.concatenate([jnp.exp2(sp - m), jnp.exp2(sc - m)], axis=1).astype(BF16)
        both = _dot(p, jnp.concatenate([vv, ones], axis=1))
        num, den = both[:, :HEAD_DIM], both[:, HEAD_DIM:]
        return num * (1.0 / den), m + jnp.log2(den)

    def put(group, rows, parts):
        o_acc[group, rows, :], l_acc[group, rows, :] = parts

    cat = lambda a, b: jnp.concatenate([a, b], axis=0)

    put(0, slice(0, SPAN), attend(q0[0, 0, 0, 0:SPAN, :], cat(ck0[...], k0[0, 0, 0, 0:SPAN, :]),
                                  cat(cv0[...], v0[0, 0, 0, 0:SPAN, :]), True))

    def block0(jb):
        r0 = jb * SPAN if isinstance(jb, int) else pl.multiple_of(jb * SPAN, SPAN)
        both = pl.ds(r0 - SPAN, 2 * SPAN)
        put(0, pl.ds(r0, SPAN),
            attend(q0[0, 0, 0, pl.ds(r0, SPAN), :], k0[0, 0, 0, both, :], v0[0, 0, 0, both, :], False))

    _unrolled_loop(1, SUPER // SPAN, UNROLL_D1, block0)
    ck0[...] = k0[0, 0, 0, SUPER - SPAN:SUPER, :]
    cv0[...] = v0[0, 0, 0, SUPER - SPAN:SUPER, :]

    dil1 = DIL_GROUPS[1][1]
    n_blk1 = SUPER // (dil1 * SPAN)

    def residue1(rho):
        for jb in range(n_blk1):
            rows = slice(jb * SPAN, (jb + 1) * SPAN)
            if jb == 0:
                kk, vv = cat(ck1[rho], k1[0, 0, rho, rows, :]), cat(cv1[rho], v1[0, 0, rho, rows, :])
            else:
                both = slice((jb - 1) * SPAN, (jb + 1) * SPAN)
                kk, vv = k1[0, 0, rho, both, :], v1[0, 0, rho, both, :]
            put(1, pl.ds(jb * dil1 * SPAN + rho, SPAN, stride=dil1),
                attend(q1[0, 0, rho, rows, :], kk, vv, jb == 0))
        last = slice((n_blk1 - 1) * SPAN, n_blk1 * SPAN)
        ck1[rho] = k1[0, 0, rho, last, :]
        cv1[rho] = v1[0, 0, rho, last, :]

    _unrolled_loop(0, dil1, UNROLL_D4, residue1)

    dil2 = DIL_GROUPS[2][1]

    def residue2(rho):
        put(2, pl.ds(rho, SPAN, stride=dil2),
            attend(q2[0, 0, rho], cat(ck2[rho], k2[0, 0, rho]), cat(cv2[rho], v2[0, 0, rho]), True))
        ck2[rho] = k2[0, 0, rho]
        cv2[rho] = v2[0, 0, rho]

    _unrolled_loop(0, dil2, UNROLL_D16, residue2)

    def mix(i, carry):
        rows = pl.ds(pl.multiple_of(i * MIX_BAND, MIX_BAND), MIX_BAND)
        ls = [l_acc[gi, rows, :] for gi in range(N_GROUPS)]
        lmax = jnp.maximum(jnp.maximum(ls[0], ls[1]), ls[2])
        es = [jnp.exp2(l - lmax) for l in ls]
        acc = es[0] * o_acc[0, rows, :] + es[1] * o_acc[1, rows, :] + es[2] * o_acc[2, rows, :]
        out_ref[0, 0, rows, :] = (acc * (1.0 / (es[0] + es[1] + es[2]))).astype(BF16)
        return carry

    lax.fori_loop(0, SUPER // MIX_BAND, mix, 0)


def _dilated_mixer(pieces, batch, seq):
    ins, specs, carries = [], [], []
    for gi, (_, dil) in enumerate(DIL_GROUPS):
        spec = pl.BlockSpec((1, 1, dil, SUPER // dil, HEAD_DIM), lambda b, h, t: (b, h, 0, t, 0))
        ins += [pieces[2 * gi], pieces[2 * gi + 1], pieces[2 * N_GROUPS + gi]]
        specs += [spec] * 3
        shape = (SPAN, HEAD_DIM) if dil == 1 else (dil, SPAN, HEAD_DIM)
        carries += [pltpu.VMEM(shape, BF16)] * 2
    return pl.pallas_call(
        _mixer_kernel,
        grid=(batch, A_HEADS, seq // SUPER),
        in_specs=specs,
        out_specs=pl.BlockSpec((1, 1, SUPER, HEAD_DIM), lambda b, h, t: (b, h, t, 0)),
        out_shape=jax.ShapeDtypeStruct((batch, A_HEADS, seq, HEAD_DIM), BF16),
        scratch_shapes=carries + [pltpu.VMEM((N_GROUPS, SUPER, HEAD_DIM), F32)] * 2,
        compiler_params=_params("parallel", "parallel", "arbitrary"),
        name="dilated_mixer",
    )(*ins)


def _even_tail_kernel(x_ref, halo_ref, g_ref, w_ref, wpool_ref, pscale_ref, mk_ref, mv_ref,
                      wout_ref, a_ref, out_ref, h_ref, p_ref, y_ref, *, tiles_per_seq):
    seq_tile = pl.program_id(0) % tiles_per_seq
    tile_rows = x_ref.shape[0]
    _rmsnorm_rows(x_ref, g_ref, h_ref)
    _project(h_ref, w_ref, p_ref)
    z0 = B_WIDTH + M_WIDTH

    def out_project(k0):
        yk = y_ref[:, k0:k0 + COL_CHUNK]
        for c in range(0, D_MODEL, COL_CHUNK):
            cols = slice(c, c + COL_CHUNK)
            base = x_ref[:, cols] if k0 == 0 else out_ref[:, cols]
            out_ref[:, cols] = base + _dot(yk, wout_ref[k0:k0 + COL_CHUNK, cols])

    for hd in range(A_HEADS):
        lanes = slice(hd * HEAD_DIM, (hd + 1) * HEAD_DIM)
        za = p_ref[:, z0 + hd * HEAD_DIM:z0 + (hd + 1) * HEAD_DIM]
        y_ref[:, lanes] = (a_ref[0, hd].astype(F32) * _silu(za)).astype(BF16)
    out_project(0)

    h_halo = _rmsnorm_bf16(halo_ref[...], g_ref[...])
    xb_halo = _dot(h_halo, w_ref[:, 0:B_WIDTH]) * (seq_tile > 0).astype(F32)
    run = jnp.concatenate([xb_halo, p_ref[:, 0:B_WIDTH]], axis=0)
    t = seq_tile * tile_rows + lax.broadcasted_iota(jnp.int32, (tile_rows, POOL_CH), 0)
    shift = 1
    for gi, w in enumerate(POOL_SIZES):
        while shift < w:
            run = run + pltpu.roll(run, shift, 0)
            shift *= 2
        lanes = slice(gi * POOL_CH, (gi + 1) * POOL_CH)
        cnt = jnp.minimum(t + 1, w).astype(F32)
        pooled = run[POOL_HALO:, lanes] / cnt - p_ref[:, lanes]
        yb = _dot(pooled.astype(BF16), wpool_ref[gi]) * pscale_ref[:, lanes]
        zb = p_ref[:, z0 + A_WIDTH + gi * POOL_CH:z0 + A_WIDTH + (gi + 1) * POOL_CH]
        y_ref[:, A_WIDTH + gi * POOL_CH:A_WIDTH + (gi + 1) * POOL_CH] = (yb * _silu(zb)).astype(BF16)
    out_project(A_WIDTH)

    m_out = _memory_attention(p_ref, B_WIDTH, mk_ref.at[0, 0], mv_ref.at[0, 0])
    for hd in range(MEM_HEADS):
        c0 = z0 + A_WIDTH + B_WIDTH + hd * HEAD_DIM
        y0 = A_WIDTH + B_WIDTH + hd * HEAD_DIM
        y_ref[:, y0:y0 + HEAD_DIM] = (m_out[hd] * _silu(p_ref[:, c0:c0 + HEAD_DIM])).astype(BF16)
    out_project(A_WIDTH + B_WIDTH)


def _even_tail(x2, g, w_rest, w_pool, pool_scale, memkv, layer, w_out, a_out, batch, seq):
    rows = x2.shape[0]
    tile_rows = TAIL_TILE
    tiles_per_seq = seq // tile_rows
    halo_blocks = tile_rows // POOL_HALO
    row_spec = lambda w: pl.BlockSpec((tile_rows, w), lambda i: (i, 0))
    halo_spec = pl.BlockSpec((POOL_HALO, D_MODEL), lambda i: (jnp.maximum(i * halo_blocks - 1, 0), 0))
    mk_spec = pl.BlockSpec((1, 1, N_MEM, M_WIDTH), lambda i: (layer, i // tiles_per_seq, 0, 0))
    mv_spec = pl.BlockSpec((1, 1, N_MEM, M_WIDTH), lambda i: (layer, i // tiles_per_seq, 0, 1))
    return pl.pallas_call(
        functools.partial(_even_tail_kernel, tiles_per_seq=tiles_per_seq),
        grid=(rows // tile_rows,),
        in_specs=[row_spec(D_MODEL), halo_spec, _const_spec((1, D_MODEL)),
                  _layer_spec((D_MODEL, EVEN_REST), layer),
                  _layer_spec((len(POOL_SIZES), POOL_CH, POOL_CH), layer),
                  _const_spec((1, B_WIDTH)), mk_spec, mv_spec, _layer_spec((EVEN_MIX, D_MODEL), layer),
                  pl.BlockSpec((1, A_HEADS, tile_rows, HEAD_DIM),
                               lambda i: (i // tiles_per_seq, 0, i % tiles_per_seq, 0))],
        out_specs=row_spec(D_MODEL),
        out_shape=jax.ShapeDtypeStruct((rows, D_MODEL), F32),
        scratch_shapes=[pltpu.VMEM((tile_rows, D_MODEL), BF16), pltpu.VMEM((tile_rows, EVEN_REST), F32),
                        pltpu.VMEM((tile_rows, EVEN_MIX), BF16)],
        compiler_params=_params("parallel"),
        name="even_tail",
    )(x2, x2, g.reshape(1, D_MODEL), w_rest, w_pool, pool_scale.reshape(1, B_WIDTH),
      memkv, memkv, w_out, a_out)


def _even_layer(x2, tabs, memkv, layer, g, w_qkv, w_rest, w_pool, pool_scale, w_out, batch, seq):
    assert all(d in (1, DEINT, DEINT * DEINT) for _, d in DIL_GROUPS)
    pieces = _qkv_project(x2, g, w_qkv, layer, tabs, batch, seq)
    a_out = _dilated_mixer(pieces, batch, seq)
    return _even_tail(x2, g, w_rest, w_pool, pool_scale, memkv, layer, w_out, a_out, batch, seq)


def _odd_kernel(x_ref, g_ref, w_ref, lng_ref, lnb_ref, ws_ref, bs_ref, mk_ref, mv_ref, wout_ref,
                fg_ref, out_ref, h_ref, p_ref, vn_ref, y_ref, *, final_norm):
    tile_rows = x_ref.shape[0]
    _rmsnorm_rows(x_ref, g_ref, h_ref)
    _project(h_ref, w_ref, p_ref)
    z0 = 2 * C_WIDTH + M_WIDTH

    for r in range(0, tile_rows, ROW_BAND):
        v = p_ref[r:r + ROW_BAND, C_WIDTH:2 * C_WIDTH]
        vc = v - jnp.mean(v, axis=-1, keepdims=True)
        var = jnp.mean(vc * vc, axis=-1, keepdims=True)
        vn_ref[r:r + ROW_BAND, :] = ((vc * lax.rsqrt(var + EPS)) * lng_ref[...] + lnb_ref[...]).astype(BF16)

    res_ref, res0 = (p_ref, C_WIDTH) if final_norm else (out_ref, 0)

    def out_project(k0):
        yk = y_ref[:, k0:k0 + COL_CHUNK]
        for c in range(0, D_MODEL, COL_CHUNK):
            dst = slice(res0 + c, res0 + c + COL_CHUNK)
            base = x_ref[:, c:c + COL_CHUNK] if k0 == 0 else res_ref[:, dst]
            res_ref[:, dst] = base + _dot(yk, wout_ref[k0:k0 + COL_CHUNK, c:c + COL_CHUNK])

    tri_r = lax.broadcasted_iota(jnp.int32, (CHUNK, CHUNK), 0)
    tri_c = lax.broadcasted_iota(jnp.int32, (CHUNK, CHUNK), 1)
    causal = tri_c <= tri_r
    for gi in range(C_GROUPS):
        lanes = slice(gi * C_CH, (gi + 1) * C_CH)
        ws = jnp.where(causal, ws_ref[gi], jnp.zeros((CHUNK, CHUNK), BF16))
        bias = bs_ref[:, gi:gi + 1]
        n_chunks = tile_rows // CHUNK
        vn = jnp.concatenate([vn_ref[n * CHUNK:(n + 1) * CHUNK, lanes] for n in range(n_chunks)], axis=1)
        mixed_all = _dot(ws, vn)
        for n in range(n_chunks):
            rows = slice(n * CHUNK, (n + 1) * CHUNK)
            mixed = mixed_all[:, n * C_CH:(n + 1) * C_CH] + bias
            gate = _silu(p_ref[rows, z0 + gi * C_CH:z0 + (gi + 1) * C_CH])
            y_ref[rows, lanes] = (p_ref[rows, lanes] * mixed * gate).astype(BF16)
        if (gi + 1) * C_CH % COL_CHUNK == 0:
            out_project((gi + 1) * C_CH - COL_CHUNK)

    m_out = _memory_attention(p_ref, 2 * C_WIDTH, mk_ref.at[0, 0], mv_ref.at[0, 0])
    for hd in range(MEM_HEADS):
        c0 = z0 + C_WIDTH + hd * HEAD_DIM
        y0 = C_WIDTH + hd * HEAD_DIM
        y_ref[:, y0:y0 + HEAD_DIM] = (m_out[hd] * _silu(p_ref[:, c0:c0 + HEAD_DIM])).astype(BF16)
    out_project(C_WIDTH)

    if final_norm:
        for r in range(0, tile_rows, ROW_BAND):
            res = p_ref[r:r + ROW_BAND, res0:res0 + D_MODEL]
            ms = jnp.mean(res * res, axis=-1, keepdims=True)
            out_ref[r:r + ROW_BAND, :] = (res * lax.rsqrt(ms + EPS)) * fg_ref[...]


def _odd_layer(x2, memkv, layer, g, w_in, ln_g, ln_b, w_s, b_s, w_out, final_g, final_norm, seq):
    rows = x2.shape[0]
    tile_rows = ODD_TILE
    tiles_per_seq = seq // tile_rows
    row_spec = pl.BlockSpec((tile_rows, D_MODEL), lambda i: (i, 0))
    mk_spec = pl.BlockSpec((1, 1, N_MEM, M_WIDTH), lambda i: (layer, i // tiles_per_seq, 0, 0))
    mv_spec = pl.BlockSpec((1, 1, N_MEM, M_WIDTH), lambda i: (layer, i // tiles_per_seq, 0, 1))
    return pl.pallas_call(
        functools.partial(_odd_kernel, final_norm=final_norm),
        grid=(rows // tile_rows,),
        in_specs=[row_spec, _const_spec((1, D_MODEL)), _layer_spec((D_MODEL, ODD_IN), layer),
                  _const_spec((1, C_WIDTH)), _const_spec((1, C_WIDTH)),
                  _layer_spec((C_GROUPS, CHUNK, CHUNK), layer), _const_spec((CHUNK, C_GROUPS)),
                  mk_spec, mv_spec, _layer_spec((ODD_MIX, D_MODEL), layer), _const_spec((1, D_MODEL))],
        out_specs=row_spec,
        out_shape=jax.ShapeDtypeStruct((rows, D_MODEL), F32),
        scratch_shapes=[pltpu.VMEM((tile_rows, D_MODEL), BF16), pltpu.VMEM((tile_rows, ODD_IN), F32),
                        pltpu.VMEM((tile_rows, C_WIDTH), BF16), pltpu.VMEM((tile_rows, ODD_MIX), BF16)],
        compiler_params=_params("parallel"),
        name="odd_layer",
    )(x2, g.reshape(1, D_MODEL), w_in, ln_g.reshape(1, C_WIDTH), ln_b.reshape(1, C_WIDTH),
      w_s, b_s.T, memkv, memkv, w_out, final_g.reshape(1, D_MODEL))


def kernel(x, mem, positions, g_mem, even_norm_g, even_w_in, even_w_pool, even_pool_scale,
           even_w_mem_kv, even_w_out, odd_norm_g, odd_w_in, odd_ln_g, odd_ln_b, odd_w_s,
           odd_b_s, odd_w_mem_kv, odd_w_out, final_norm_g):
    batch, seq, _ = x.shape
    depth = even_norm_g.shape[0] + odd_norm_g.shape[0]
    assert all(seq % t == 0 for t in (QKV_TILE, TAIL_TILE, ODD_TILE, SUPER))

    tabs = _rope_tables(positions)
    memkv_even = _memory_kv(mem, g_mem, even_w_mem_kv)
    memkv_odd = _memory_kv(mem, g_mem, odd_w_mem_kv)
    even_w_qkv, even_w_rest = _even_weights(even_w_in)
    even_w_pool, even_w_out, odd_w_in, odd_w_s, odd_w_out = (
        w.astype(BF16) for w in (even_w_pool, even_w_out, odd_w_in, odd_w_s, odd_w_out))

    x2 = x.reshape(batch * seq, D_MODEL)
    for layer in range(depth):
        i = layer // 2
        if layer % 2 == 0:
            x2 = _even_layer(x2, tabs, memkv_even, i, even_norm_g[i], even_w_qkv, even_w_rest, even_w_pool,
                             even_pool_scale[i], even_w_out, batch, seq)
        else:
            x2 = _odd_layer(x2, memkv_odd, i, odd_norm_g[i], odd_w_in, odd_ln_g[i], odd_ln_b[i],
                            odd_w_s, odd_b_s[i], odd_w_out, final_norm_g,
                            layer == depth - 1, seq)
    return x2.reshape(batch, seq, D_MODEL)
```

```python
import functools
import math

import jax
import jax.numpy as jnp
from jax import lax
from jax.experimental import pallas as pl
from jax.experimental.pallas import tpu as pltpu

D_MODEL = 1024
HEAD_DIM = 128
EPS = 1e-6
ROPE_THETA = 500000.0
ROT_DIM = HEAD_DIM // 4
ROT_HALF = ROT_DIM // 2
N_MEM = 256
MEM_HEADS = 4
DIL_GROUPS = ((128, 1), (512, 4), (2048, 16))
A_HEADS = 4
POOL_SIZES = (2, 4, 8, 16)
POOL_CH = 128
CHUNK = 128
C_GROUPS = 8
C_CH = 128

A_WIDTH = A_HEADS * HEAD_DIM
B_WIDTH = len(POOL_SIZES) * POOL_CH
C_WIDTH = C_GROUPS * C_CH
M_WIDTH = MEM_HEADS * HEAD_DIM
EVEN_MIX = A_WIDTH + B_WIDTH + M_WIDTH
ODD_MIX = C_WIDTH + M_WIDTH
N_GROUPS = len(DIL_GROUPS)
A_QK_WIDTH = 2 * N_GROUPS * A_WIDTH
QKV_WIDTH = A_QK_WIDTH + A_WIDTH
EVEN_IN = QKV_WIDTH + B_WIDTH + M_WIDTH + EVEN_MIX
EVEN_REST = EVEN_IN - QKV_WIDTH
ODD_IN = 2 * C_WIDTH + M_WIDTH + ODD_MIX
SPAN = 128
SCALE = HEAD_DIM ** -0.5
SCALE_LOG2 = SCALE * math.log2(math.e)
POOL_HALO = max(POOL_SIZES)
SUPER = max(d for _, d in DIL_GROUPS) * SPAN
MIX_BAND = 256
UNROLL_D1 = 15
UNROLL_D4 = 4
UNROLL_D16 = 16

LANES = 128
QKV_TILE = 1024
TAIL_TILE = 1024
ODD_TILE = 1024
ROW_BAND = 128
ROPE_BAND = 64
ROPE_APART = 64
DEINT = 4
WEIGHT_ROWS = 256
COL_CHUNK = 512
VMEM_LIMIT = 56 * 1024 * 1024

F32 = jnp.float32
BF16 = jnp.bfloat16


def _const_spec(shape):
    nd = len(shape)
    return pl.BlockSpec(shape, lambda *_: (0,) * nd, pipeline_mode=pl.Buffered(1))


def _layer_spec(shape, layer):
    nd = len(shape)
    return pl.BlockSpec((None,) + tuple(shape), lambda *_: (layer,) + (0,) * nd,
                        pipeline_mode=pl.Buffered(1))


def _params(*sem):
    return pltpu.CompilerParams(dimension_semantics=sem, vmem_limit_bytes=VMEM_LIMIT)


def _rmsnorm_bf16(x, g):
    ms = jnp.mean(x * x, axis=-1, keepdims=True)
    return ((x * lax.rsqrt(ms + EPS)) * g).astype(BF16)


def _rmsnorm_rows(x_ref, g_ref, h_ref):
    g = g_ref[...]
    for r in range(0, x_ref.shape[0], ROW_BAND):
        h_ref[r:r + ROW_BAND, :] = _rmsnorm_bf16(x_ref[r:r + ROW_BAND, :], g)


def _project(h_ref, w_ref, p_ref, col0=0):
    for c in range(0, p_ref.shape[1], COL_CHUNK):
        p_ref[:, c:c + COL_CHUNK] = _dot(h_ref[...], w_ref[:, col0 + c:col0 + c + COL_CHUNK])


def _unrolled_loop(lo, hi, unroll, fn):
    n = hi - lo
    assert n % unroll == 0
    if unroll == n:
        for i in range(lo, hi):
            fn(i)
        return

    def trip(t, carry):
        for s in range(unroll):
            fn(lo + t * unroll + s)
        return carry

    lax.fori_loop(0, n // unroll, trip, 0)


def _dot(a, b):
    return jnp.dot(a, b, preferred_element_type=F32)


def _dot_nt(a, b):
    return lax.dot_general(a, b, (((1,), (1,)), ((), ())), preferred_element_type=F32)


def _silu(z):
    return z * (1.0 / (1.0 + jnp.exp(-z)))


ROPE_PACK = ROPE_APART // ROT_HALF
ROPE_ROWS = 256


def _rope_kernel(pos_ref, inv_ref, sign_ref, c_ref, s_ref):
    ang = pos_ref[...].astype(F32) * inv_ref[...]
    cos = jnp.cos(ang)
    sin = jnp.sin(ang) * sign_ref[...]
    lane = lax.broadcasted_iota(jnp.int32, cos.shape, 1)
    rotary = (lane % ROPE_APART) < ROT_HALF
    for j in range(ROPE_PACK):
        shift = (LANES - ROT_HALF * j) % LANES
        cj = cos if shift == 0 else pltpu.roll(cos, shift, 1)
        sj = sin if shift == 0 else pltpu.roll(sin, shift, 1)
        rows = pl.ds(j, ROPE_ROWS, stride=ROPE_PACK)
        c_ref[rows, :] = jnp.where(rotary, cj, 1.0)
        s_ref[rows, :] = jnp.where(rotary, sj, 0.0)


def _rope_tables(positions):
    n = positions.size
    inv = ROPE_THETA ** (-jnp.arange(0, ROT_DIM, 2, dtype=F32) / ROT_DIM)
    slots = jnp.repeat(positions.reshape(n // ROPE_PACK, ROPE_PACK), ROT_HALF, axis=1)
    pos_c = jnp.tile(slots, (1, LANES // ROPE_APART))
    inv_c = jnp.tile(inv, LANES // ROT_HALF).reshape(1, LANES)
    sign = jnp.where(jnp.arange(LANES) < ROPE_APART, -1.0, 1.0).astype(F32).reshape(1, LANES)
    lane_spec = _const_spec((1, LANES))
    out_spec = pl.BlockSpec((ROPE_ROWS * ROPE_PACK, LANES), lambda i: (i, 0))
    return pl.pallas_call(
        _rope_kernel,
        grid=(n // (ROPE_ROWS * ROPE_PACK),),
        in_specs=[pl.BlockSpec((ROPE_ROWS, LANES), lambda i: (i, 0)), lane_spec, lane_spec],
        out_specs=[out_spec, out_spec],
        out_shape=(jax.ShapeDtypeStruct((n, LANES), F32),) * 2,
        compiler_params=_params("parallel"),
        name="rope_tables",
    )(pos_c, inv_c, sign)


def _even_weights_kernel(w_ref, qkv_ref, rest_ref):
    lane = lax.broadcasted_iota(jnp.int32, (w_ref.shape[0], HEAD_DIM), 1)
    low = (lane >= ROT_HALF) & (lane < ROT_DIM)
    high = (lane >= ROPE_APART) & (lane < ROPE_APART + ROT_HALF)
    gap = ROPE_APART - ROT_HALF
    for hd in range(A_QK_WIDTH // HEAD_DIM):
        lanes = slice(hd * HEAD_DIM, (hd + 1) * HEAD_DIM)
        w = w_ref[:, lanes]
        w = jnp.where(low, pltpu.roll(w, HEAD_DIM - gap, 1), jnp.where(high, pltpu.roll(w, gap, 1), w))
        qkv_ref[:, lanes] = w.astype(BF16)
    qkv_ref[:, A_QK_WIDTH:] = w_ref[:, A_QK_WIDTH:QKV_WIDTH].astype(BF16)
    rest_ref[...] = w_ref[:, QKV_WIDTH:].astype(BF16)


def _even_weights(w_in):
    n_layers = w_in.shape[0]
    spec = lambda w: pl.BlockSpec((None, WEIGHT_ROWS, w), lambda l, r: (l, r, 0))
    return pl.pallas_call(
        _even_weights_kernel,
        grid=(n_layers, D_MODEL // WEIGHT_ROWS),
        in_specs=[spec(EVEN_IN)],
        out_specs=[spec(QKV_WIDTH), spec(EVEN_REST)],
        out_shape=[jax.ShapeDtypeStruct((n_layers, D_MODEL, QKV_WIDTH), BF16),
                   jax.ShapeDtypeStruct((n_layers, D_MODEL, EVEN_REST), BF16)],
        compiler_params=_params("parallel", "parallel"),
        name="even_weights",
    )(w_in)


def _memkv_kernel(mem_ref, g_ref, w_ref, out_ref):
    for r in range(0, mem_ref.shape[0], N_MEM):
        mem_n = _rmsnorm_bf16(mem_ref[r:r + N_MEM, :], g_ref[...])
        out_ref[0, r:r + N_MEM, :] = _dot(mem_n, w_ref[0]).astype(BF16)


def _memory_kv(mem, g_mem, w_kv):
    n_layers = w_kv.shape[0]
    batch = mem.shape[0]
    rows = batch * N_MEM
    out = pl.pallas_call(
        _memkv_kernel,
        grid=(n_layers,),
        in_specs=[_const_spec((rows, D_MODEL)), _const_spec((1, D_MODEL)),
                  pl.BlockSpec((1, D_MODEL, 2 * M_WIDTH), lambda l: (l, 0, 0))],
        out_specs=pl.BlockSpec((1, rows, 2 * M_WIDTH), lambda l: (l, 0, 0)),
        out_shape=jax.ShapeDtypeStruct((n_layers, rows, 2 * M_WIDTH), BF16),
        compiler_params=_params("parallel"),
        name="memory_kv",
    )(mem.reshape(rows, D_MODEL), g_mem.reshape(1, D_MODEL), w_kv.astype(BF16))
    return out.reshape(n_layers, batch, N_MEM, 2 * M_WIDTH)


def _memory_attention(p_ref, col0, mk_ref, mv_ref):
    outs = []
    ones = jnp.ones((N_MEM, LANES), BF16)
    for h in range(MEM_HEADS):
        lanes = slice(h * HEAD_DIM, (h + 1) * HEAD_DIM)
        q = p_ref[:, col0 + h * HEAD_DIM:col0 + (h + 1) * HEAD_DIM].astype(BF16)
        sc = _dot_nt(q, mk_ref[:, lanes]) * SCALE_LOG2
        e = jnp.exp2(sc - jnp.max(sc, axis=-1, keepdims=True)).astype(BF16)
        both = _dot(e, jnp.concatenate([mv_ref[:, lanes], ones], axis=1))
        outs.append(both[:, :HEAD_DIM] * (1.0 / both[:, HEAD_DIM:]))
    return outs


def _qkv_kernel(x_ref, g_ref, w_ref, c_ref, s_ref, *refs):
    n_qk = 2 * N_GROUPS
    qk_refs, v_refs = refs[:n_qk], refs[n_qk:n_qk + N_GROUPS]
    h_ref, acc_ref, st_ref = refs[n_qk + N_GROUPS:]
    heads = COL_CHUNK // HEAD_DIM
    tile_rows = x_ref.shape[0]
    _rmsnorm_rows(x_ref, g_ref, h_ref)

    def by_residue(head, targets):
        for b in range(DEINT):
            part = acc_ref[head, pl.ds(b, tile_rows // DEINT, stride=DEINT), :]
            for ref, dil in targets:
                if dil == DEINT:
                    ref[0, head, b] = part.astype(BF16)
                else:
                    st_ref[head, b] = part
                    for a in range(DEINT):
                        ref[0, head, DEINT * a + b] = (
                            st_ref[head, b, pl.ds(a, tile_rows // dil, stride=DEINT), :].astype(BF16))

    for c in range(QKV_WIDTH // COL_CHUNK):
        res = _dot(h_ref[...], w_ref[:, c * COL_CHUNK:(c + 1) * COL_CHUNK])
        for hh in range(heads):
            acc_ref[hh] = res[:, hh * HEAD_DIM:(hh + 1) * HEAD_DIM]
        if c < n_qk:
            ref, dil = qk_refs[c], DIL_GROUPS[c // 2][1]
            for r in range(0, tile_rows, ROPE_BAND):
                rows = slice(r, r + ROPE_BAND)
                c_tab, s_tab = c_ref[rows, :], s_ref[rows, :]
                for hh in range(heads):
                    blk = acc_ref[hh, rows, :]
                    blk = blk * c_tab + pltpu.roll(blk, ROPE_APART, 1) * s_tab
                    if dil == 1:
                        ref[0, hh, 0, rows, :] = blk.astype(BF16)
                    else:
                        acc_ref[hh, rows, :] = blk
            if dil > 1:
                for hh in range(heads):
                    by_residue(hh, [(ref, dil)])
        else:
            for hh in range(heads):
                v_refs[0][0, hh, 0] = acc_ref[hh].astype(BF16)
                by_residue(hh, [(v_refs[gi], DIL_GROUPS[gi][1]) for gi in range(1, N_GROUPS)])


def _qkv_project(x2, g, w_qkv, layer, tabs, batch, seq):
    rows = x2.shape[0]
    tile = QKV_TILE
    tiles_per_seq = seq // tile
    row_spec = lambda w: pl.BlockSpec((tile, w), lambda i: (i, 0))
    dils = [DIL_GROUPS[c // 2][1] for c in range(2 * N_GROUPS)] + [d for _, d in DIL_GROUPS]
    out_spec = lambda d: pl.BlockSpec((1, A_HEADS, d, tile // d, HEAD_DIM),
                                      lambda i: (i // tiles_per_seq, 0, 0, i % tiles_per_seq, 0))
    out_shape = lambda d: jax.ShapeDtypeStruct((batch, A_HEADS, d, seq // d, HEAD_DIM), BF16)
    return pl.pallas_call(
        _qkv_kernel,
        grid=(rows // tile,),
        in_specs=[row_spec(D_MODEL), _const_spec((1, D_MODEL)), _layer_spec((D_MODEL, QKV_WIDTH), layer),
                  row_spec(LANES), row_spec(LANES)],
        out_specs=[out_spec(d) for d in dils],
        out_shape=[out_shape(d) for d in dils],
        scratch_shapes=[pltpu.VMEM((tile, D_MODEL), BF16),
                        pltpu.VMEM((COL_CHUNK // HEAD_DIM, tile, HEAD_DIM), F32),
                        pltpu.VMEM((COL_CHUNK // HEAD_DIM, DEINT, tile // DEINT, HEAD_DIM), F32)],
        compiler_params=_params("parallel"),
        name="even_qkv",
    )(x2, g.reshape(1, D_MODEL), w_qkv, *tabs)


def _mixer_kernel(q0, k0, v0, q1, k1, v1, q2, k2, v2, out_ref,
                  ck0, cv0, ck1, cv1, ck2, cv2, o_acc, l_acc):
    tile = pl.program_id(2)
    carries = (ck0, cv0, ck1, cv1, ck2, cv2)

    @pl.when(tile == 0)
    def _():
        for c in carries:
            c[...] = jnp.zeros_like(c)

    no_prev = jnp.where(tile > 0, 0.0, -jnp.inf).astype(F32)
    row = lax.broadcasted_iota(jnp.int32, (SPAN, SPAN), 0)
    col = lax.broadcasted_iota(jnp.int32, (SPAN, SPAN), 1)
    tri_prev = col >= row
    tri_cur = col <= row
    ones = jnp.ones((2 * SPAN, LANES), BF16)

    def attend(q, kk, vv, first):
        s = _dot_nt(q, kk) * SCALE_LOG2
        sp, sc = s[:, :SPAN], s[:, SPAN:]
        if first:
            sp = sp + no_prev
        sp = jnp.where(tri_prev, sp, -jnp.inf)
        sc = jnp.where(tri_cur, sc, -jnp.inf)
        m = jnp.max(jnp.maximum(sp, sc), axis=-1, keepdims=True)
        p = jnp.concatenate([jnp.exp2(sp - m), jnp.exp2(sc - m)], axis=1).astype(BF16)
        both = _dot(p, jnp.concatenate([vv, ones], axis=1))
        num, den = both[:, :HEAD_DIM], both[:, HEAD_DIM:]
        return num * (1.0 / den), m + jnp.log2(den)

    def put(group, rows, parts):
        o_acc[group, rows, :], l_acc[group, rows, :] = parts

    cat = lambda a, b: jnp.concatenate([a, b], axis=0)

    put(0, slice(0, SPAN), attend(q0[0, 0, 0, 0:SPAN, :], cat(ck0[...], k0[0, 0, 0, 0:SPAN, :]),
                                  cat(cv0[...], v0[0, 0, 0, 0:SPAN, :]), True))

    def block0(jb):
        r0 = jb * SPAN if isinstance(jb, int) else pl.multiple_of(jb * SPAN, SPAN)
        both = pl.ds(r0 - SPAN, 2 * SPAN)
        put(0, pl.ds(r0, SPAN),
            attend(q0[0, 0, 0, pl.ds(r0, SPAN), :], k0[0, 0, 0, both, :], v0[0, 0, 0, both, :], False))

    _unrolled_loop(1, SUPER // SPAN, UNROLL_D1, block0)
    ck0[...] = k0[0, 0, 0, SUPER - SPAN:SUPER, :]
    cv0[...] = v0[0, 0, 0, SUPER - SPAN:SUPER, :]

    dil1 = DIL_GROUPS[1][1]
    n_blk1 = SUPER // (dil1 * SPAN)

    def residue1(rho):
        for jb in range(n_blk1):
            rows = slice(jb * SPAN, (jb + 1) * SPAN)
            if jb == 0:
                kk, vv = cat(ck1[rho], k1[0, 0, rho, rows, :]), cat(cv1[rho], v1[0, 0, rho, rows, :])
            else:
                both = slice((jb - 1) * SPAN, (jb + 1) * SPAN)
                kk, vv = k1[0, 0, rho, both, :], v1[0, 0, rho, both, :]
            put(1, pl.ds(jb * dil1 * SPAN + rho, SPAN, stride=dil1),
                attend(q1[0, 0, rho, rows, :], kk, vv, jb == 0))
        last = slice((n_blk1 - 1) * SPAN, n_blk1 * SPAN)
        ck1[rho] = k1[0, 0, rho, last, :]
        cv1[rho] = v1[0, 0, rho, last, :]

    _unrolled_loop(0, dil1, UNROLL_D4, residue1)

    dil2 = DIL_GROUPS[2][1]

    def residue2(rho):
        put(2, pl.ds(rho, SPAN, stride=dil2),
            attend(q2[0, 0, rho], cat(ck2[rho], k2[0, 0, rho]), cat(cv2[rho], v2[0, 0, rho]), True))
        ck2[rho] = k2[0, 0, rho]
        cv2[rho] = v2[0, 0, rho]

    _unrolled_loop(0, dil2, UNROLL_D16, residue2)

    def mix(i, carry):
        rows = pl.ds(pl.multiple_of(i * MIX_BAND, MIX_BAND), MIX_BAND)
        ls = [l_acc[gi, rows, :] for gi in range(N_GROUPS)]
        lmax = jnp.maximum(jnp.maximum(ls[0], ls[1]), ls[2])
        es = [jnp.exp2(l - lmax) for l in ls]
        acc = es[0] * o_acc[0, rows, :] + es[1] * o_acc[1, rows, :] + es[2] * o_acc[2, rows, :]
        out_ref[0, 0, rows, :] = (acc * (1.0 / (es[0] + es[1] + es[2]))).astype(BF16)
        return carry

    lax.fori_loop(0, SUPER // MIX_BAND, mix, 0)


def _dilated_mixer(pieces, batch, seq):
    ins, specs, carries = [], [], []
    for gi, (_, dil) in enumerate(DIL_GROUPS):
        spec = pl.BlockSpec((1, 1, dil, SUPER // dil, HEAD_DIM), lambda b, h, t: (b, h, 0, t, 0))
        ins += [pieces[2 * gi], pieces[2 * gi + 1], pieces[2 * N_GROUPS + gi]]
        specs += [spec] * 3
        shape = (SPAN, HEAD_DIM) if dil == 1 else (dil, SPAN, HEAD_DIM)
        carries += [pltpu.VMEM(shape, BF16)] * 2
    return pl.pallas_call(
        _mixer_kernel,
        grid=(batch, A_HEADS, seq // SUPER),
        in_specs=specs,
        out_specs=pl.BlockSpec((1, 1, SUPER, HEAD_DIM), lambda b, h, t: (b, h, t, 0)),
        out_shape=jax.ShapeDtypeStruct((batch, A_HEADS, seq, HEAD_DIM), BF16),
        scratch_shapes=carries + [pltpu.VMEM((N_GROUPS, SUPER, HEAD_DIM), F32)] * 2,
        compiler_params=_params("parallel", "parallel", "arbitrary"),
        name="dilated_mixer",
    )(*ins)


def _even_tail_kernel(x_ref, halo_ref, g_ref, w_ref, wpool_ref, pscale_ref, mk_ref, mv_ref,
                      wout_ref, a_ref, out_ref, h_ref, p_ref, y_ref, *, tiles_per_seq):
    seq_tile = pl.program_id(0) % tiles_per_seq
    tile_rows = x_ref.shape[0]
    _rmsnorm_rows(x_ref, g_ref, h_ref)
    _project(h_ref, w_ref, p_ref)
    z0 = B_WIDTH + M_WIDTH

    def out_project(k0):
        yk = y_ref[:, k0:k0 + COL_CHUNK]
        for c in range(0, D_MODEL, COL_CHUNK):
            cols = slice(c, c + COL_CHUNK)
            base = x_ref[:, cols] if k0 == 0 else out_ref[:, cols]
            out_ref[:, cols] = base + _dot(yk, wout_ref[k0:k0 + COL_CHUNK, cols])

    for hd in range(A_HEADS):
        lanes = slice(hd * HEAD_DIM, (hd + 1) * HEAD_DIM)
        za = p_ref[:, z0 + hd * HEAD_DIM:z0 + (hd + 1) * HEAD_DIM]
        y_ref[:, lanes] = (a_ref[0, hd].astype(F32) * _silu(za)).astype(BF16)
    out_project(0)

    h_halo = _rmsnorm_bf16(halo_ref[...], g_ref[...])
    xb_halo = _dot(h_halo, w_ref[:, 0:B_WIDTH]) * (seq_tile > 0).astype(F32)
    run = jnp.concatenate([xb_halo, p_ref[:, 0:B_WIDTH]], axis=0)
    t = seq_tile * tile_rows + lax.broadcasted_iota(jnp.int32, (tile_rows, POOL_CH), 0)
    shift = 1
    for gi, w in enumerate(POOL_SIZES):
        while shift < w:
            run = run + pltpu.roll(run, shift, 0)
            shift *= 2
        lanes = slice(gi * POOL_CH, (gi + 1) * POOL_CH)
        cnt = jnp.minimum(t + 1, w).astype(F32)
        pooled = run[POOL_HALO:, lanes] / cnt - p_ref[:, lanes]
        yb = _dot(pooled.astype(BF16), wpool_ref[gi]) * pscale_ref[:, lanes]
        zb = p_ref[:, z0 + A_WIDTH + gi * POOL_CH:z0 + A_WIDTH + (gi + 1) * POOL_CH]
        y_ref[:, A_WIDTH + gi * POOL_CH:A_WIDTH + (gi + 1) * POOL_CH] = (yb * _silu(zb)).astype(BF16)
    out_project(A_WIDTH)

    m_out = _memory_attention(p_ref, B_WIDTH, mk_ref.at[0, 0], mv_ref.at[0, 0])
    for hd in range(MEM_HEADS):
        c0 = z0 + A_WIDTH + B_WIDTH + hd * HEAD_DIM
        y0 = A_WIDTH + B_WIDTH + hd * HEAD_DIM
        y_ref[:, y0:y0 + HEAD_DIM] = (m_out[hd] * _silu(p_ref[:, c0:c0 + HEAD_DIM])).astype(BF16)
    out_project(A_WIDTH + B_WIDTH)


def _even_tail(x2, g, w_rest, w_pool, pool_scale, memkv, layer, w_out, a_out, batch, seq):
    rows = x2.shape[0]
    tile_rows = TAIL_TILE
    tiles_per_seq = seq // tile_rows
    halo_blocks = tile_rows // POOL_HALO
    row_spec = lambda w: pl.BlockSpec((tile_rows, w), lambda i: (i, 0))
    halo_spec = pl.BlockSpec((POOL_HALO, D_MODEL), lambda i: (jnp.maximum(i * halo_blocks - 1, 0), 0))
    mk_spec = pl.BlockSpec((1, 1, N_MEM, M_WIDTH), lambda i: (layer, i // tiles_per_seq, 0, 0))
    mv_spec = pl.BlockSpec((1, 1, N_MEM, M_WIDTH), lambda i: (layer, i // tiles_per_seq, 0, 1))
    return pl.pallas_call(
        functools.partial(_even_tail_kernel, tiles_per_seq=tiles_per_seq),
        grid=(rows // tile_rows,),
        in_specs=[row_spec(D_MODEL), halo_spec, _const_spec((1, D_MODEL)),
                  _layer_spec((D_MODEL, EVEN_REST), layer),
                  _layer_spec((len(POOL_SIZES), POOL_CH, POOL_CH), layer),
                  _const_spec((1, B_WIDTH)), mk_spec, mv_spec, _layer_spec((EVEN_MIX, D_MODEL), layer),
                  pl.BlockSpec((1, A_HEADS, tile_rows, HEAD_DIM),
                               lambda i: (i // tiles_per_seq, 0, i % tiles_per_seq, 0))],
        out_specs=row_spec(D_MODEL),
        out_shape=jax.ShapeDtypeStruct((rows, D_MODEL), F32),
        scratch_shapes=[pltpu.VMEM((tile_rows, D_MODEL), BF16), pltpu.VMEM((tile_rows, EVEN_REST), F32),
                        pltpu.VMEM((tile_rows, EVEN_MIX), BF16)],
        compiler_params=_params("parallel"),
        name="even_tail",
    )(x2, x2, g.reshape(1, D_MODEL), w_rest, w_pool, pool_scale.reshape(1, B_WIDTH),
      memkv, memkv, w_out, a_out)


def _even_layer(x2, tabs, memkv, layer, g, w_qkv, w_rest, w_pool, pool_scale, w_out, batch, seq):
    assert all(d in (1, DEINT, DEINT * DEINT) for _, d in DIL_GROUPS)
    pieces = _qkv_project(x2, g, w_qkv, layer, tabs, batch, seq)
    a_out = _dilated_mixer(pieces, batch, seq)
    return _even_tail(x2, g, w_rest, w_pool, pool_scale, memkv, layer, w_out, a_out, batch, seq)


def _odd_kernel(x_ref, g_ref, w_ref, lng_ref, lnb_ref, ws_ref, bs_ref, mk_ref, mv_ref, wout_ref,
                fg_ref, out_ref, h_ref, p_ref, vn_ref, y_ref, *, final_norm):
    tile_rows = x_ref.shape[0]
    _rmsnorm_rows(x_ref, g_ref, h_ref)
    _project(h_ref, w_ref, p_ref)
    z0 = 2 * C_WIDTH + M_WIDTH

    for r in range(0, tile_rows, ROW_BAND):
        v = p_ref[r:r + ROW_BAND, C_WIDTH:2 * C_WIDTH]
        vc = v - jnp.mean(v, axis=-1, keepdims=True)
        var = jnp.mean(vc * vc, axis=-1, keepdims=True)
        vn_ref[r:r + ROW_BAND, :] = ((vc * lax.rsqrt(var + EPS)) * lng_ref[...] + lnb_ref[...]).astype(BF16)

    res_ref, res0 = (p_ref, C_WIDTH) if final_norm else (out_ref, 0)

    def out_project(k0):
        yk = y_ref[:, k0:k0 + COL_CHUNK]
        for c in range(0, D_MODEL, COL_CHUNK):
            dst = slice(res0 + c, res0 + c + COL_CHUNK)
            base = x_ref[:, c:c + COL_CHUNK] if k0 == 0 else res_ref[:, dst]
            res_ref[:, dst] = base + _dot(yk, wout_ref[k0:k0 + COL_CHUNK, c:c + COL_CHUNK])

    tri_r = lax.broadcasted_iota(jnp.int32, (CHUNK, CHUNK), 0)
    tri_c = lax.broadcasted_iota(jnp.int32, (CHUNK, CHUNK), 1)
    causal = tri_c <= tri_r
    for gi in range(C_GROUPS):
        lanes = slice(gi * C_CH, (gi + 1) * C_CH)
        ws = jnp.where(causal, ws_ref[gi], jnp.zeros((CHUNK, CHUNK), BF16))
        bias = bs_ref[:, gi:gi + 1]
        n_chunks = tile_rows // CHUNK
        vn = jnp.concatenate([vn_ref[n * CHUNK:(n + 1) * CHUNK, lanes] for n in range(n_chunks)], axis=1)
        mixed_all = _dot(ws, vn)
        for n in range(n_chunks):
            rows = slice(n * CHUNK, (n + 1) * CHUNK)
            mixed = mixed_all[:, n * C_CH:(n + 1) * C_CH] + bias
            gate = _silu(p_ref[rows, z0 + gi * C_CH:z0 + (gi + 1) * C_CH])
            y_ref[rows, lanes] = (p_ref[rows, lanes] * mixed * gate).astype(BF16)
        if (gi + 1) * C_CH % COL_CHUNK == 0:
            out_project((gi + 1) * C_CH - COL_CHUNK)

    m_out = _memory_attention(p_ref, 2 * C_WIDTH, mk_ref.at[0, 0], mv_ref.at[0, 0])
    for hd in range(MEM_HEADS):
        c0 = z0 + C_WIDTH + hd * HEAD_DIM
        y0 = C_WIDTH + hd * HEAD_DIM
        y_ref[:, y0:y0 + HEAD_DIM] = (m_out[hd] * _silu(p_ref[:, c0:c0 + HEAD_DIM])).astype(BF16)
    out_project(C_WIDTH)

    if final_norm:
        for r in range(0, tile_rows, ROW_BAND):
            res = p_ref[r:r + ROW_BAND, res0:res0 + D_MODEL]
            ms = jnp.mean(res * res, axis=-1, keepdims=True)
            out_ref[r:r + ROW_BAND, :] = (res * lax.rsqrt(ms + EPS)) * fg_ref[...]


def _odd_layer(x2, memkv, layer, g, w_in, ln_g, ln_b, w_s, b_s, w_out, final_g, final_norm, seq):
    rows = x2.shape[0]
    tile_rows = ODD_TILE
    tiles_per_seq = seq // tile_rows
    row_spec = pl.BlockSpec((tile_rows, D_MODEL), lambda i: (i, 0))
    mk_spec = pl.BlockSpec((1, 1, N_MEM, M_WIDTH), lambda i: (layer, i // tiles_per_seq, 0, 0))
    mv_spec = pl.BlockSpec((1, 1, N_MEM, M_WIDTH), lambda i: (layer, i // tiles_per_seq, 0, 1))
    return pl.pallas_call(
        functools.partial(_odd_kernel, final_norm=final_norm),
        grid=(rows // tile_rows,),
        in_specs=[row_spec, _const_spec((1, D_MODEL)), _layer_spec((D_MODEL, ODD_IN), layer),
                  _const_spec((1, C_WIDTH)), _const_spec((1, C_WIDTH)),
                  _layer_spec((C_GROUPS, CHUNK, CHUNK), layer), _const_spec((CHUNK, C_GROUPS)),
                  mk_spec, mv_spec, _layer_spec((ODD_MIX, D_MODEL), layer), _const_spec((1, D_MODEL))],
        out_specs=row_spec,
        out_shape=jax.ShapeDtypeStruct((rows, D_MODEL), F32),
        scratch_shapes=[pltpu.VMEM((tile_rows, D_MODEL), BF16), pltpu.VMEM((tile_rows, ODD_IN), F32),
                        pltpu.VMEM((tile_rows, C_WIDTH), BF16), pltpu.VMEM((tile_rows, ODD_MIX), BF16)],
        compiler_params=_params("parallel"),
        name="odd_layer",
    )(x2, g.reshape(1, D_MODEL), w_in, ln_g.reshape(1, C_WIDTH), ln_b.reshape(1, C_WIDTH),
      w_s, b_s.T, memkv, memkv, w_out, final_g.reshape(1, D_MODEL))


def kernel(x, mem, positions, g_mem, even_norm_g, even_w_in, even_w_pool, even_pool_scale,
           even_w_mem_kv, even_w_out, odd_norm_g, odd_w_in, odd_ln_g, odd_ln_b, odd_w_s,
           odd_b_s, odd_w_mem_kv, odd_w_out, final_norm_g):
    batch, seq, _ = x.shape
    depth = even_norm_g.shape[0] + odd_norm_g.shape[0]
    assert all(seq % t == 0 for t in (QKV_TILE, TAIL_TILE, ODD_TILE, SUPER))

    tabs = _rope_tables(positions)
    memkv_even = _memory_kv(mem, g_mem, even_w_mem_kv)
    memkv_odd = _memory_kv(mem, g_mem, odd_w_mem_kv)
    even_w_qkv, even_w_rest = _even_weights(even_w_in)
    even_w_pool, even_w_out, odd_w_in, odd_w_s, odd_w_out = (
        w.astype(BF16) for w in (even_w_pool, even_w_out, odd_w_in, odd_w_s, odd_w_out))

    x2 = x.reshape(batch * seq, D_MODEL)
    for layer in range(depth):
        i = layer // 2
        if layer % 2 == 0:
            x2 = _even_layer(x2, tabs, memkv_even, i, even_norm_g[i], even_w_qkv, even_w_rest, even_w_pool,
                             even_pool_scale[i], even_w_out, batch, seq)
        else:
            x2 = _odd_layer(x2, memkv_odd, i, odd_norm_g[i], odd_w_in, odd_ln_g[i], odd_ln_b[i],
                            odd_w_s, odd_b_s[i], odd_w_out, final_norm_g,
                            layer == depth - 1, seq)
    return x2.reshape(batch, seq, D_MODEL)
```

```python
import functools
import math

import jax
import jax.numpy as jnp
from jax import lax
from jax.experimental import pallas as pl
from jax.experimental.pallas import tpu as pltpu

D_MODEL = 1024
HEAD_DIM = 128
EPS = 1e-6
ROPE_THETA = 500000.0
ROT_DIM = HEAD_DIM // 4
ROT_HALF = ROT_DIM // 2
N_MEM = 256
MEM_HEADS = 4
DIL_GROUPS = ((128, 1), (512, 4), (2048, 16))
A_HEADS = 4
POOL_SIZES = (2, 4, 8, 16)
POOL_CH = 128
CHUNK = 128
C_GROUPS = 8
C_CH = 128

A_WIDTH = A_HEADS * HEAD_DIM
B_WIDTH = len(POOL_SIZES) * POOL_CH
C_WIDTH = C_GROUPS * C_CH
M_WIDTH = MEM_HEADS * HEAD_DIM
EVEN_MIX = A_WIDTH + B_WIDTH + M_WIDTH
ODD_MIX = C_WIDTH + M_WIDTH
N_GROUPS = len(DIL_GROUPS)
A_QK_WIDTH = 2 * N_GROUPS * A_WIDTH
QKV_WIDTH = A_QK_WIDTH + A_WIDTH
EVEN_IN = QKV_WIDTH + B_WIDTH + M_WIDTH + EVEN_MIX
EVEN_REST = EVEN_IN - QKV_WIDTH
ODD_IN = 2 * C_WIDTH + M_WIDTH + ODD_MIX
SPAN = 128
SCALE = HEAD_DIM ** -0.5
SCALE_LOG2 = SCALE * math.log2(math.e)
POOL_HALO = max(POOL_SIZES)
SUPER = max(d for _, d in DIL_GROUPS) * SPAN
MIX_BAND = 256
UNROLL_D1 = 15
UNROLL_D4 = 4
UNROLL_D16 = 16

LANES = 128
QKV_TILE = 1024
TAIL_TILE = 1024
ODD_TILE = 1024
ROW_BAND = 128
ROPE_BAND = 64
ROPE_APART = 64
DEINT = 4
WEIGHT_ROWS = 256
COL_CHUNK = 512
VMEM_LIMIT = 56 * 1024 * 1024

F32 = jnp.float32
BF16 = jnp.bfloat16


def _const_spec(shape):
    nd = len(shape)
    return pl.BlockSpec(shape, lambda *_: (0,) * nd, pipeline_mode=pl.Buffered(1))


def _layer_spec(shape, layer):
    nd = len(shape)
    return pl.BlockSpec((None,) + tuple(shape), lambda *_: (layer,) + (0,) * nd,
                        pipeline_mode=pl.Buffered(1))


def _params(*sem):
    return pltpu.CompilerParams(dimension_semantics=sem, vmem_limit_bytes=VMEM_LIMIT)


def _rmsnorm_bf16(x, g):
    ms = jnp.mean(x * x, axis=-1, keepdims=True)
    return ((x * lax.rsqrt(ms + EPS)) * g).astype(BF16)


def _rmsnorm_rows(x_ref, g_ref, h_ref):
    g = g_ref[...]
    for r in range(0, x_ref.shape[0], ROW_BAND):
        h_ref[r:r + ROW_BAND, :] = _rmsnorm_bf16(x_ref[r:r + ROW_BAND, :], g)


def _project(h_ref, w_ref, p_ref, col0=0):
    for c in range(0, p_ref.shape[1], COL_CHUNK):
        p_ref[:, c:c + COL_CHUNK] = _dot(h_ref[...], w_ref[:, col0 + c:col0 + c + COL_CHUNK])


def _unrolled_loop(lo, hi, unroll, fn):
    n = hi - lo
    assert n % unroll == 0
    if unroll == n:
        for i in range(lo, hi):
            fn(i)
        return

    def trip(t, carry):
        for s in range(unroll):
            fn(lo + t * unroll + s)
        return carry

    lax.fori_loop(0, n // unroll, trip, 0)


def _dot(a, b):
    return jnp.dot(a, b, preferred_element_type=F32)


def _dot_nt(a, b):
    return lax.dot_general(a, b, (((1,), (1,)), ((), ())), preferred_element_type=F32)


def _silu(z):
    return z * (1.0 / (1.0 + jnp.exp(-z)))


ROPE_PACK = ROPE_APART // ROT_HALF
ROPE_ROWS = 256


def _rope_kernel(pos_ref, inv_ref, sign_ref, c_ref, s_ref):
    ang = pos_ref[...].astype(F32) * inv_ref[...]
    cos = jnp.cos(ang)
    sin = jnp.sin(ang) * sign_ref[...]
    lane = lax.broadcasted_iota(jnp.int32, cos.shape, 1)
    rotary = (lane % ROPE_APART) < ROT_HALF
    for j in range(ROPE_PACK):
        shift = (LANES - ROT_HALF * j) % LANES
        cj = cos if shift == 0 else pltpu.roll(cos, shift, 1)
        sj = sin if shift == 0 else pltpu.roll(sin, shift, 1)
        rows = pl.ds(j, ROPE_ROWS, stride=ROPE_PACK)
        c_ref[rows, :] = jnp.where(rotary, cj, 1.0)
        s_ref[rows, :] = jnp.where(rotary, sj, 0.0)


def _rope_tables(positions):
    n = positions.size
    inv = ROPE_THETA ** (-jnp.arange(0, ROT_DIM, 2, dtype=F32) / ROT_DIM)
    slots = jnp.repeat(positions.reshape(n // ROPE_PACK, ROPE_PACK), ROT_HALF, axis=1)
    pos_c = jnp.tile(slots, (1, LANES // ROPE_APART))
    inv_c = jnp.tile(inv, LANES // ROT_HALF).reshape(1, LANES)
    sign = jnp.where(jnp.arange(LANES) < ROPE_APART, -1.0, 1.0).astype(F32).reshape(1, LANES)
    lane_spec = _const_spec((1, LANES))
    out_spec = pl.BlockSpec((ROPE_ROWS * ROPE_PACK, LANES), lambda i: (i, 0))
    return pl.pallas_call(
        _rope_kernel,
        grid=(n // (ROPE_ROWS * ROPE_PACK),),
        in_specs=[pl.BlockSpec((ROPE_ROWS, LANES), lambda i: (i, 0)), lane_spec, lane_spec],
        out_specs=[out_spec, out_spec],
        out_shape=(jax.ShapeDtypeStruct((n, LANES), F32),) * 2,
        compiler_params=_params("parallel"),
        name="rope_tables",
    )(pos_c, inv_c, sign)


def _even_weights_kernel(w_ref, qkv_ref, rest_ref):
    lane = lax.broadcasted_iota(jnp.int32, (w_ref.shape[0], HEAD_DIM), 1)
    low = (lane >= ROT_HALF) & (lane < ROT_DIM)
    high = (lane >= ROPE_APART) & (lane < ROPE_APART + ROT_HALF)
    gap = ROPE_APART - ROT_HALF
    for hd in range(A_QK_WIDTH // HEAD_DIM):
        lanes = slice(hd * HEAD_DIM, (hd + 1) * HEAD_DIM)
        w = w_ref[:, lanes]
        w = jnp.where(low, pltpu.roll(w, HEAD_DIM - gap, 1), jnp.where(high, pltpu.roll(w, gap, 1), w))
        qkv_ref[:, lanes] = w.astype(BF16)
    qkv_ref[:, A_QK_WIDTH:] = w_ref[:, A_QK_WIDTH:QKV_WIDTH].astype(BF16)
    rest_ref[...] = w_ref[:, QKV_WIDTH:].astype(BF16)


def _even_weights(w_in):
    n_layers = w_in.shape[0]
    spec = lambda w: pl.BlockSpec((None, WEIGHT_ROWS, w), lambda l, r: (l, r, 0))
    return pl.pallas_call(
        _even_weights_kernel,
        grid=(n_layers, D_MODEL // WEIGHT_ROWS),
        in_specs=[spec(EVEN_IN)],
        out_specs=[spec(QKV_WIDTH), spec(EVEN_REST)],
        out_shape=[jax.ShapeDtypeStruct((n_layers, D_MODEL, QKV_WIDTH), BF16),
                   jax.ShapeDtypeStruct((n_layers, D_MODEL, EVEN_REST), BF16)],
        compiler_params=_params("parallel", "parallel"),
        name="even_weights",
    )(w_in)


def _memkv_kernel(mem_ref, g_ref, w_ref, out_ref):
    for r in range(0, mem_ref.shape[0], N_MEM):
        mem_n = _rmsnorm_bf16(mem_ref[r:r + N_MEM, :], g_ref[...])
        out_ref[0, r:r + N_MEM, :] = _dot(mem_n, w_ref[0]).astype(BF16)


def _memory_kv(mem, g_mem, w_kv):
    n_layers = w_kv.shape[0]
    batch = mem.shape[0]
    rows = batch * N_MEM
    out = pl.pallas_call(
        _memkv_kernel,
        grid=(n_layers,),
        in_specs=[_const_spec((rows, D_MODEL)), _const_spec((1, D_MODEL)),
                  pl.BlockSpec((1, D_MODEL, 2 * M_WIDTH), lambda l: (l, 0, 0))],
        out_specs=pl.BlockSpec((1, rows, 2 * M_WIDTH), lambda l: (l, 0, 0)),
        out_shape=jax.ShapeDtypeStruct((n_layers, rows, 2 * M_WIDTH), BF16),
        compiler_params=_params("parallel"),
        name="memory_kv",
    )(mem.reshape(rows, D_MODEL), g_mem.reshape(1, D_MODEL), w_kv.astype(BF16))
    return out.reshape(n_layers, batch, N_MEM, 2 * M_WIDTH)


def _memory_attention(p_ref, col0, mk_ref, mv_ref):
    outs = []
    ones = jnp.ones((N_MEM, LANES), BF16)
    for h in range(MEM_HEADS):
        lanes = slice(h * HEAD_DIM, (h + 1) * HEAD_DIM)
        q = p_ref[:, col0 + h * HEAD_DIM:col0 + (h + 1) * HEAD_DIM].astype(BF16)
        sc = _dot_nt(q, mk_ref[:, lanes]) * SCALE_LOG2
        e = jnp.exp2(sc - jnp.max(sc, axis=-1, keepdims=True)).astype(BF16)
        both = _dot(e, jnp.concatenate([mv_ref[:, lanes], ones], axis=1))
        outs.append(both[:, :HEAD_DIM] * (1.0 / both[:, HEAD_DIM:]))
    return outs


def _qkv_kernel(x_ref, g_ref, w_ref, c_ref, s_ref, *refs):
    n_qk = 2 * N_GROUPS
    group_refs = refs[:N_GROUPS]
    h_ref, acc_ref, st_ref = refs[N_GROUPS:]
    heads = COL_CHUNK // HEAD_DIM
    tile_rows = x_ref.shape[0]
    _rmsnorm_rows(x_ref, g_ref, h_ref)

    def by_residue(head, targets):
        for b in range(DEINT):
            part = acc_ref[head, pl.ds(b, tile_rows // DEINT, stride=DEINT), :]
            for ref, slot, dil in targets:
                if dil == DEINT:
                    ref[0, slot, head, b] = part.astype(BF16)
                else:
                    st_ref[head, b] = part
                    for a in range(DEINT):
                        ref[0, slot, head, DEINT * a + b] = (
                            st_ref[head, b, pl.ds(a, tile_rows // dil, stride=DEINT), :].astype(BF16))

    for c in range(QKV_WIDTH // COL_CHUNK):
        res = _dot(h_ref[...], w_ref[:, c * COL_CHUNK:(c + 1) * COL_CHUNK])
        for hh in range(heads):
            acc_ref[hh] = res[:, hh * HEAD_DIM:(hh + 1) * HEAD_DIM]
        if c < n_qk:
            ref, slot, dil = group_refs[c // 2], c % 2, DIL_GROUPS[c // 2][1]
            for r in range(0, tile_rows, ROPE_BAND):
                rows = slice(r, r + ROPE_BAND)
                c_tab, s_tab = c_ref[rows, :], s_ref[rows, :]
                for hh in range(heads):
                    blk = acc_ref[hh, rows, :]
                    blk = blk * c_tab + pltpu.roll(blk, ROPE_APART, 1) * s_tab
                    if dil == 1:
                        ref[0, slot, hh, 0, rows, :] = blk.astype(BF16)
                    else:
                        acc_ref[hh, rows, :] = blk
            if dil > 1:
                for hh in range(heads):
                    by_residue(hh, [(ref, slot, dil)])
        else:
            for hh in range(heads):
                group_refs[0][0, 2, hh, 0] = acc_ref[hh].astype(BF16)
                by_residue(hh, [(group_refs[gi], 2, DIL_GROUPS[gi][1]) for gi in range(1, N_GROUPS)])


def _qkv_project(x2, g, w_qkv, layer, tabs, batch, seq):
    rows = x2.shape[0]
    tile = QKV_TILE
    tiles_per_seq = seq // tile
    row_spec = lambda w: pl.BlockSpec((tile, w), lambda i: (i, 0))
    dils = [d for _, d in DIL_GROUPS]
    out_spec = lambda d: pl.BlockSpec((1, 3, A_HEADS, d, tile // d, HEAD_DIM),
                                      lambda i: (i // tiles_per_seq, 0, 0, 0, i % tiles_per_seq, 0))
    out_shape = lambda d: jax.ShapeDtypeStruct((batch, 3, A_HEADS, d, seq // d, HEAD_DIM), BF16)
    return pl.pallas_call(
        _qkv_kernel,
        grid=(rows // tile,),
        in_specs=[row_spec(D_MODEL), _const_spec((1, D_MODEL)), _layer_spec((D_MODEL, QKV_WIDTH), layer),
                  row_spec(LANES), row_spec(LANES)],
        out_specs=[out_spec(d) for d in dils],
        out_shape=[out_shape(d) for d in dils],
        scratch_shapes=[pltpu.VMEM((tile, D_MODEL), BF16),
                        pltpu.VMEM((COL_CHUNK // HEAD_DIM, tile, HEAD_DIM), F32),
                        pltpu.VMEM((COL_CHUNK // HEAD_DIM, DEINT, tile // DEINT, HEAD_DIM), F32)],
        compiler_params=_params("parallel"),
        name="even_qkv",
    )(x2, g.reshape(1, D_MODEL), w_qkv, *tabs)


def _mixer_kernel(g0, g1, g2, out_ref, ck0, cv0, ck1, cv1, ck2, cv2, o_acc, l_acc):
    tile = pl.program_id(2)
    carries = (ck0, cv0, ck1, cv1, ck2, cv2)
    (q0, k0, v0), (q1, k1, v1), (q2, k2, v2) = ([g.at[0, s, 0] for s in range(3)] for g in (g0, g1, g2))

    @pl.when(tile == 0)
    def _():
        for c in carries:
            c[...] = jnp.zeros_like(c)

    no_prev = jnp.where(tile > 0, 0.0, -jnp.inf).astype(F32)
    row = lax.broadcasted_iota(jnp.int32, (SPAN, SPAN), 0)
    col = lax.broadcasted_iota(jnp.int32, (SPAN, SPAN), 1)
    tri_prev = col >= row
    tri_cur = col <= row
    ones = jnp.ones((2 * SPAN, LANES), BF16)

    def attend(q, kk, vv, first):
        s = _dot_nt(q, kk) * SCALE_LOG2
        sp, sc = s[:, :SPAN], s[:, SPAN:]
        if first:
            sp = sp + no_prev
        sp = jnp.where(tri_prev, sp, -jnp.inf)
        sc = jnp.where(tri_cur, sc, -jnp.inf)
        m = jnp.max(jnp.maximum(sp, sc), axis=-1, keepdims=True)
        p = jnp.concatenate([jnp.exp2(sp - m), jnp.exp2(sc - m)], axis=1).astype(BF16)
        both = _dot(p, jnp.concatenate([vv, ones], axis=1))
        num, den = both[:, :HEAD_DIM], both[:, HEAD_DIM:]
        return num * (1.0 / den), m + jnp.log2(den)

    def put(group, rows, parts):
        o_acc[group, rows, :], l_acc[group, rows, :] = parts

    cat = lambda a, b: jnp.concatenate([a, b], axis=0)

    put(0, slice(0, SPAN), attend(q0[0, 0:SPAN, :], cat(ck0[...], k0[0, 0:SPAN, :]),
                                  cat(cv0[...], v0[0, 0:SPAN, :]), True))

    def block0(jb):
        r0 = jb * SPAN if isinstance(jb, int) else pl.multiple_of(jb * SPAN, SPAN)
        both = pl.ds(r0 - SPAN, 2 * SPAN)
        put(0, pl.ds(r0, SPAN),
            attend(q0[0, pl.ds(r0, SPAN), :], k0[0, both, :], v0[0, both, :], False))

    _unrolled_loop(1, SUPER // SPAN, UNROLL_D1, block0)
    ck0[...] = k0[0, SUPER - SPAN:SUPER, :]
    cv0[...] = v0[0, SUPER - SPAN:SUPER, :]

    dil1 = DIL_GROUPS[1][1]
    n_blk1 = SUPER // (dil1 * SPAN)

    def residue1(rho):
        for jb in range(n_blk1):
            rows = slice(jb * SPAN, (jb + 1) * SPAN)
            if jb == 0:
                kk, vv = cat(ck1[rho], k1[rho, rows, :]), cat(cv1[rho], v1[rho, rows, :])
            else:
                both = slice((jb - 1) * SPAN, (jb + 1) * SPAN)
                kk, vv = k1[rho, both, :], v1[rho, both, :]
            put(1, pl.ds(jb * dil1 * SPAN + rho, SPAN, stride=dil1),
                attend(q1[rho, rows, :], kk, vv, jb == 0))
        last = slice((n_blk1 - 1) * SPAN, n_blk1 * SPAN)
        ck1[rho] = k1[rho, last, :]
        cv1[rho] = v1[rho, last, :]

    _unrolled_loop(0, dil1, UNROLL_D4, residue1)

    dil2 = DIL_GROUPS[2][1]

    def residue2(rho):
        put(2, pl.ds(rho, SPAN, stride=dil2),
            attend(q2[rho], cat(ck2[rho], k2[rho]), cat(cv2[rho], v2[rho]), True))
        ck2[rho] = k2[rho]
        cv2[rho] = v2[rho]

    _unrolled_loop(0, dil2, UNROLL_D16, residue2)

    def mix(i, carry):
        rows = pl.ds(pl.multiple_of(i * MIX_BAND, MIX_BAND), MIX_BAND)
        ls = [l_acc[gi, rows, :] for gi in range(N_GROUPS)]
        lmax = jnp.maximum(jnp.maximum(ls[0], ls[1]), ls[2])
        es = [jnp.exp2(l - lmax) for l in ls]
        acc = es[0] * o_acc[0, rows, :] + es[1] * o_acc[1, rows, :] + es[2] * o_acc[2, rows, :]
        out_ref[0, 0, rows, :] = (acc * (1.0 / (es[0] + es[1] + es[2]))).astype(BF16)
        return carry

    lax.fori_loop(0, SUPER // MIX_BAND, mix, 0)


def _dilated_mixer(pieces, batch, seq):
    specs, carries = [], []
    for _, dil in DIL_GROUPS:
        specs.append(pl.BlockSpec((1, 3, 1, dil, SUPER // dil, HEAD_DIM), lambda b, h, t: (b, 0, h, 0, t, 0)))
        shape = (SPAN, HEAD_DIM) if dil == 1 else (dil, SPAN, HEAD_DIM)
        carries += [pltpu.VMEM(shape, BF16)] * 2
    return pl.pallas_call(
        _mixer_kernel,
        grid=(batch, A_HEADS, seq // SUPER),
        in_specs=specs,
        out_specs=pl.BlockSpec((1, 1, SUPER, HEAD_DIM), lambda b, h, t: (b, h, t, 0)),
        out_shape=jax.ShapeDtypeStruct((batch, A_HEADS, seq, HEAD_DIM), BF16),
        scratch_shapes=carries + [pltpu.VMEM((N_GROUPS, SUPER, HEAD_DIM), F32)] * 2,
        compiler_params=_params("parallel", "parallel", "arbitrary"),
        name="dilated_mixer",
    )(*pieces)


def _even_tail_kernel(x_ref, halo_ref, g_ref, w_ref, wpool_ref, pscale_ref, mk_ref, mv_ref,
                      wout_ref, a_ref, out_ref, h_ref, p_ref, y_ref, *, tiles_per_seq):
    seq_tile = pl.program_id(0) % tiles_per_seq
    tile_rows = x_ref.shape[0]
    _rmsnorm_rows(x_ref, g_ref, h_ref)
    _project(h_ref, w_ref, p_ref)
    z0 = B_WIDTH + M_WIDTH

    def out_project(k0):
        yk = y_ref[:, k0:k0 + COL_CHUNK]
        for c in range(0, D_MODEL, COL_CHUNK):
            cols = slice(c, c + COL_CHUNK)
            base = x_ref[:, cols] if k0 == 0 else out_ref[:, cols]
            out_ref[:, cols] = base + _dot(yk, wout_ref[k0:k0 + COL_CHUNK, cols])

    for hd in range(A_HEADS):
        lanes = slice(hd * HEAD_DIM, (hd + 1) * HEAD_DIM)
        za = p_ref[:, z0 + hd * HEAD_DIM:z0 + (hd + 1) * HEAD_DIM]
        y_ref[:, lanes] = (a_ref[0, hd].astype(F32) * _silu(za)).astype(BF16)
    out_project(0)

    h_halo = _rmsnorm_bf16(halo_ref[...], g_ref[...])
    xb_halo = _dot(h_halo, w_ref[:, 0:B_WIDTH]) * (seq_tile > 0).astype(F32)
    run = jnp.concatenate([xb_halo, p_ref[:, 0:B_WIDTH]], axis=0)
    t = seq_tile * tile_rows + lax.broadcasted_iota(jnp.int32, (tile_rows, POOL_CH), 0)
    shift = 1
    for gi, w in enumerate(POOL_SIZES):
        while shift < w:
            run = run + pltpu.roll(run, shift, 0)
            shift *= 2
        lanes = slice(gi * POOL_CH, (gi + 1) * POOL_CH)
        cnt = jnp.minimum(t + 1, w).astype(F32)
        pooled = run[POOL_HALO:, lanes] / cnt - p_ref[:, lanes]
        yb = _dot(pooled.astype(BF16), wpool_ref[gi]) * pscale_ref[:, lanes]
        zb = p_ref[:, z0 + A_WIDTH + gi * POOL_CH:z0 + A_WIDTH + (gi + 1) * POOL_CH]
        y_ref[:, A_WIDTH + gi * POOL_CH:A_WIDTH + (gi + 1) * POOL_CH] = (yb * _silu(zb)).astype(BF16)
    out_project(A_WIDTH)

    m_out = _memory_attention(p_ref, B_WIDTH, mk_ref.at[0, 0], mv_ref.at[0, 0])
    for hd in range(MEM_HEADS):
        c0 = z0 + A_WIDTH + B_WIDTH + hd * HEAD_DIM
        y0 = A_WIDTH + B_WIDTH + hd * HEAD_DIM
        y_ref[:, y0:y0 + HEAD_DIM] = (m_out[hd] * _silu(p_ref[:, c0:c0 + HEAD_DIM])).astype(BF16)
    out_project(A_WIDTH + B_WIDTH)


def _even_tail(x2, g, w_rest, w_pool, pool_scale, memkv, layer, w_out, a_out, batch, seq):
    rows = x2.shape[0]
    tile_rows = TAIL_TILE
    tiles_per_seq = seq // tile_rows
    halo_blocks = tile_rows // POOL_HALO
    row_spec = lambda w: pl.BlockSpec((tile_rows, w), lambda i: (i, 0))
    halo_spec = pl.BlockSpec((POOL_HALO, D_MODEL), lambda i: (jnp.maximum(i * halo_blocks - 1, 0), 0))
    mk_spec = pl.BlockSpec((1, 1, N_MEM, M_WIDTH), lambda i: (layer, i // tiles_per_seq, 0, 0))
    mv_spec = pl.BlockSpec((1, 1, N_MEM, M_WIDTH), lambda i: (layer, i // tiles_per_seq, 0, 1))
    return pl.pallas_call(
        functools.partial(_even_tail_kernel, tiles_per_seq=tiles_per_seq),
        grid=(rows // tile_rows,),
        in_specs=[row_spec(D_MODEL), halo_spec, _const_spec((1, D_MODEL)),
                  _layer_spec((D_MODEL, EVEN_REST), layer),
                  _layer_spec((len(POOL_SIZES), POOL_CH, POOL_CH), layer),
                  _const_spec((1, B_WIDTH)), mk_spec, mv_spec, _layer_spec((EVEN_MIX, D_MODEL), layer),
                  pl.BlockSpec((1, A_HEADS, tile_rows, HEAD_DIM),
                               lambda i: (i // tiles_per_seq, 0, i % tiles_per_seq, 0))],
        out_specs=row_spec(D_MODEL),
        out_shape=jax.ShapeDtypeStruct((rows, D_MODEL), F32),
        scratch_shapes=[pltpu.VMEM((tile_rows, D_MODEL), BF16), pltpu.VMEM((tile_rows, EVEN_REST), F32),
                        pltpu.VMEM((tile_rows, EVEN_MIX), BF16)],
        compiler_params=_params("parallel"),
        name="even_tail",
    )(x2, x2, g.reshape(1, D_MODEL), w_rest, w_pool, pool_scale.reshape(1, B_WIDTH),
      memkv, memkv, w_out, a_out)


def _even_layer(x2, tabs, memkv, layer, g, w_qkv, w_rest, w_pool, pool_scale, w_out, batch, seq):
    assert all(d in (1, DEINT, DEINT * DEINT) for _, d in DIL_GROUPS)
    pieces = _qkv_project(x2, g, w_qkv, layer, tabs, batch, seq)
    a_out = _dilated_mixer(pieces, batch, seq)
    return _even_tail(x2, g, w_rest, w_pool, pool_scale, memkv, layer, w_out, a_out, batch, seq)


def _odd_kernel(x_ref, g_ref, w_ref, lng_ref, lnb_ref, ws_ref, bs_ref, mk_ref, mv_ref, wout_ref,
                fg_ref, out_ref, h_ref, p_ref, vn_ref, y_ref, *, final_norm):
    tile_rows = x_ref.shape[0]
    _rmsnorm_rows(x_ref, g_ref, h_ref)
    _project(h_ref, w_ref, p_ref)
    z0 = 2 * C_WIDTH + M_WIDTH

    for r in range(0, tile_rows, ROW_BAND):
        v = p_ref[r:r + ROW_BAND, C_WIDTH:2 * C_WIDTH]
        vc = v - jnp.mean(v, axis=-1, keepdims=True)
        var = jnp.mean(vc * vc, axis=-1, keepdims=True)
        vn_ref[r:r + ROW_BAND, :] = ((vc * lax.rsqrt(var + EPS)) * lng_ref[...] + lnb_ref[...]).astype(BF16)

    res_ref, res0 = (p_ref, C_WIDTH) if final_norm else (out_ref, 0)

    def out_project(k0):
        yk = y_ref[:, k0:k0 + COL_CHUNK]
        for c in range(0, D_MODEL, COL_CHUNK):
            dst = slice(res0 + c, res0 + c + COL_CHUNK)
            base = x_ref[:, c:c + COL_CHUNK] if k0 == 0 else res_ref[:, dst]
            res_ref[:, dst] = base + _dot(yk, wout_ref[k0:k0 + COL_CHUNK, c:c + COL_CHUNK])

    tri_r = lax.broadcasted_iota(jnp.int32, (CHUNK, CHUNK), 0)
    tri_c = lax.broadcasted_iota(jnp.int32, (CHUNK, CHUNK), 1)
    causal = tri_c <= tri_r
    for gi in range(C_GROUPS):
        lanes = slice(gi * C_CH, (gi + 1) * C_CH)
        ws = jnp.where(causal, ws_ref[gi], jnp.zeros((CHUNK, CHUNK), BF16))
        bias = bs_ref[:, gi:gi + 1]
        n_chunks = tile_rows // CHUNK
        vn = jnp.concatenate([vn_ref[n * CHUNK:(n + 1) * CHUNK, lanes] for n in range(n_chunks)], axis=1)
        mixed_all = _dot(ws, vn)
        for n in range(n_chunks):
            rows = slice(n * CHUNK, (n + 1) * CHUNK)
            mixed = mixed_all[:, n * C_CH:(n + 1) * C_CH] + bias
            gate = _silu(p_ref[rows, z0 + gi * C_CH:z0 + (gi + 1) * C_CH])
            y_ref[rows, lanes] = (p_ref[rows, lanes] * mixed * gate).astype(BF16)
        if (gi + 1) * C_CH % COL_CHUNK == 0:
            out_project((gi + 1) * C_CH - COL_CHUNK)

    m_out = _memory_attention(p_ref, 2 * C_WIDTH, mk_ref.at[0, 0], mv_ref.at[0, 0])
    for hd in range(MEM_HEADS):
        c0 = z0 + C_WIDTH + hd * HEAD_DIM
        y0 = C_WIDTH + hd * HEAD_DIM
        y_ref[:, y0:y0 + HEAD_DIM] = (m_out[hd] * _silu(p_ref[:, c0:c0 + HEAD_DIM])).astype(BF16)
    out_project(C_WIDTH)

    if final_norm:
        for r in range(0, tile_rows, ROW_BAND):
            res = p_ref[r:r + ROW_BAND, res0:res0 + D_MODEL]
            ms = jnp.mean(res * res, axis=-1, keepdims=True)
            out_ref[r:r + ROW_BAND, :] = (res * lax.rsqrt(ms + EPS)) * fg_ref[...]


def _odd_layer(x2, memkv, layer, g, w_in, ln_g, ln_b, w_s, b_s, w_out, final_g, final_norm, seq):
    rows = x2.shape[0]
    tile_rows = ODD_TILE
    tiles_per_seq = seq // tile_rows
    row_spec = pl.BlockSpec((tile_rows, D_MODEL), lambda i: (i, 0))
    mk_spec = pl.BlockSpec((1, 1, N_MEM, M_WIDTH), lambda i: (layer, i // tiles_per_seq, 0, 0))
    mv_spec = pl.BlockSpec((1, 1, N_MEM, M_WIDTH), lambda i: (layer, i // tiles_per_seq, 0, 1))
    return pl.pallas_call(
        functools.partial(_odd_kernel, final_norm=final_norm),
        grid=(rows // tile_rows,),
        in_specs=[row_spec, _const_spec((1, D_MODEL)), _layer_spec((D_MODEL, ODD_IN), layer),
                  _const_spec((1, C_WIDTH)), _const_spec((1, C_WIDTH)),
                  _layer_spec((C_GROUPS, CHUNK, CHUNK), layer), _const_spec((CHUNK, C_GROUPS)),
                  mk_spec, mv_spec, _layer_spec((ODD_MIX, D_MODEL), layer), _const_spec((1, D_MODEL))],
        out_specs=row_spec,
        out_shape=jax.ShapeDtypeStruct((rows, D_MODEL), F32),
        scratch_shapes=[pltpu.VMEM((tile_rows, D_MODEL), BF16), pltpu.VMEM((tile_rows, ODD_IN), F32),
                        pltpu.VMEM((tile_rows, C_WIDTH), BF16), pltpu.VMEM((tile_rows, ODD_MIX), BF16)],
        compiler_params=_params("parallel"),
        name="odd_layer",
    )(x2, g.reshape(1, D_MODEL), w_in, ln_g.reshape(1, C_WIDTH), ln_b.reshape(1, C_WIDTH),
      w_s, b_s.T, memkv, memkv, w_out, final_g.reshape(1, D_MODEL))


def kernel(x, mem, positions, g_mem, even_norm_g, even_w_in, even_w_pool, even_pool_scale,
           even_w_mem_kv, even_w_out, odd_norm_g, odd_w_in, odd_ln_g, odd_ln_b, odd_w_s,
           odd_b_s, odd_w_mem_kv, odd_w_out, final_norm_g):
    batch, seq, _ = x.shape
    depth = even_norm_g.shape[0] + odd_norm_g.shape[0]
    assert all(seq % t == 0 for t in (QKV_TILE, TAIL_TILE, ODD_TILE, SUPER))

    tabs = _rope_tables(positions)
    memkv_even = _memory_kv(mem, g_mem, even_w_mem_kv)
    memkv_odd = _memory_kv(mem, g_mem, odd_w_mem_kv)
    even_w_qkv, even_w_rest = _even_weights(even_w_in)
    even_w_pool, even_w_out, odd_w_in, odd_w_s, odd_w_out = (
        w.astype(BF16) for w in (even_w_pool, even_w_out, odd_w_in, odd_w_s, odd_w_out))

    x2 = x.reshape(batch * seq, D_MODEL)
    for layer in range(depth):
        i = layer // 2
        if layer % 2 == 0:
            x2 = _even_layer(x2, tabs, memkv_even, i, even_norm_g[i], even_w_qkv, even_w_rest, even_w_pool,
                             even_pool_scale[i], even_w_out, batch, seq)
        else:
            x2 = _odd_layer(x2, memkv_odd, i, odd_norm_g[i], odd_w_in, odd_ln_g[i], odd_ln_b[i],
                            odd_w_s, odd_b_s[i], odd_w_out, final_norm_g,
                            layer == depth - 1, seq)
    return x2.reshape(batch, seq, D_MODEL)
```

```python
import functools
import math

import jax
import jax.numpy as jnp
from jax import lax
from jax.experimental import pallas as pl
from jax.experimental.pallas import tpu as pltpu

D_MODEL = 1024
HEAD_DIM = 128
EPS = 1e-6
ROPE_THETA = 500000.0
ROT_DIM = HEAD_DIM // 4
ROT_HALF = ROT_DIM // 2
N_MEM = 256
MEM_HEADS = 4
DIL_GROUPS = ((128, 1), (512, 4), (2048, 16))
A_HEADS = 4
POOL_SIZES = (2, 4, 8, 16)
POOL_CH = 128
CHUNK = 128
C_GROUPS = 8
C_CH = 128

A_WIDTH = A_HEADS * HEAD_DIM
B_WIDTH = len(POOL_SIZES) * POOL_CH
C_WIDTH = C_GROUPS * C_CH
M_WIDTH = MEM_HEADS * HEAD_DIM
EVEN_MIX = A_WIDTH + B_WIDTH + M_WIDTH
ODD_MIX = C_WIDTH + M_WIDTH
N_GROUPS = len(DIL_GROUPS)
A_QK_WIDTH = 2 * N_GROUPS * A_WIDTH
QKV_WIDTH = A_QK_WIDTH + A_WIDTH
EVEN_IN = QKV_WIDTH + B_WIDTH + M_WIDTH + EVEN_MIX
EVEN_REST = EVEN_IN - QKV_WIDTH
ODD_IN = 2 * C_WIDTH + M_WIDTH + ODD_MIX
SPAN = 128
SCALE = HEAD_DIM ** -0.5
SCALE_LOG2 = SCALE * math.log2(math.e)
POOL_HALO = max(POOL_SIZES)
SUPER = max(d for _, d in DIL_GROUPS) * SPAN
MIX_BAND = 256
LSE_LANES = 32
UNROLL_D1 = 15
UNROLL_D4 = 4
UNROLL_D16 = 16

LANES = 128
QKV_TILE = 1024
TAIL_TILE = 1024
ODD_TILE = 1024
ROW_BAND = 128
ROPE_BAND = 64
ROPE_APART = 64
DEINT = 4
WEIGHT_ROWS = 256
COL_CHUNK = 512
VMEM_LIMIT = 56 * 1024 * 1024

F32 = jnp.float32
BF16 = jnp.bfloat16


def _const_spec(shape):
    nd = len(shape)
    return pl.BlockSpec(shape, lambda *_: (0,) * nd, pipeline_mode=pl.Buffered(1))


def _layer_spec(shape, layer):
    nd = len(shape)
    return pl.BlockSpec((None,) + tuple(shape), lambda *_: (layer,) + (0,) * nd,
                        pipeline_mode=pl.Buffered(1))


def _params(*sem):
    return pltpu.CompilerParams(dimension_semantics=sem, vmem_limit_bytes=VMEM_LIMIT)


def _rmsnorm_bf16(x, g):
    ms = jnp.mean(x * x, axis=-1, keepdims=True)
    return ((x * lax.rsqrt(ms + EPS)) * g).astype(BF16)


def _rmsnorm_rows(x_ref, g_ref, h_ref):
    g = g_ref[...]
    for r in range(0, x_ref.shape[0], ROW_BAND):
        h_ref[r:r + ROW_BAND, :] = _rmsnorm_bf16(x_ref[r:r + ROW_BAND, :], g)


def _project(h_ref, w_ref, p_ref, col0=0):
    for c in range(0, p_ref.shape[1], COL_CHUNK):
        p_ref[:, c:c + COL_CHUNK] = _dot(h_ref[...], w_ref[:, col0 + c:col0 + c + COL_CHUNK])


def _unrolled_loop(lo, hi, unroll, fn):
    n = hi - lo
    assert n % unroll == 0
    if unroll == n:
        for i in range(lo, hi):
            fn(i)
        return

    def trip(t, carry):
        for s in range(unroll):
            fn(lo + t * unroll + s)
        return carry

    lax.fori_loop(0, n // unroll, trip, 0)


def _dot(a, b):
    return jnp.dot(a, b, preferred_element_type=F32)


def _dot_nt(a, b):
    return lax.dot_general(a, b, (((1,), (1,)), ((), ())), preferred_element_type=F32)


def _silu(z):
    return z * (1.0 / (1.0 + jnp.exp(-z)))


ROPE_PACK = ROPE_APART // ROT_HALF
ROPE_ROWS = 256


def _rope_kernel(pos_ref, inv_ref, sign_ref, c_ref, s_ref):
    ang = pos_ref[...].astype(F32) * inv_ref[...]
    cos = jnp.cos(ang)
    sin = jnp.sin(ang) * sign_ref[...]
    lane = lax.broadcasted_iota(jnp.int32, cos.shape, 1)
    rotary = (lane % ROPE_APART) < ROT_HALF
    for j in range(ROPE_PACK):
        shift = (LANES - ROT_HALF * j) % LANES
        cj = cos if shift == 0 else pltpu.roll(cos, shift, 1)
        sj = sin if shift == 0 else pltpu.roll(sin, shift, 1)
        rows = pl.ds(j, ROPE_ROWS, stride=ROPE_PACK)
        c_ref[rows, :] = jnp.where(rotary, cj, 1.0)
        s_ref[rows, :] = jnp.where(rotary, sj, 0.0)


def _rope_tables(positions):
    n = positions.size
    inv = ROPE_THETA ** (-jnp.arange(0, ROT_DIM, 2, dtype=F32) / ROT_DIM)
    slots = jnp.repeat(positions.reshape(n // ROPE_PACK, ROPE_PACK), ROT_HALF, axis=1)
    pos_c = jnp.tile(slots, (1, LANES // ROPE_APART))
    inv_c = jnp.tile(inv, LANES // ROT_HALF).reshape(1, LANES)
    sign = jnp.where(jnp.arange(LANES) < ROPE_APART, -1.0, 1.0).astype(F32).reshape(1, LANES)
    lane_spec = _const_spec((1, LANES))
    out_spec = pl.BlockSpec((ROPE_ROWS * ROPE_PACK, LANES), lambda i: (i, 0))
    return pl.pallas_call(
        _rope_kernel,
        grid=(n // (ROPE_ROWS * ROPE_PACK),),
        in_specs=[pl.BlockSpec((ROPE_ROWS, LANES), lambda i: (i, 0)), lane_spec, lane_spec],
        out_specs=[out_spec, out_spec],
        out_shape=(jax.ShapeDtypeStruct((n, LANES), F32),) * 2,
        compiler_params=_params("parallel"),
        name="rope_tables",
    )(pos_c, inv_c, sign)


def _even_weights_kernel(w_ref, qkv_ref, rest_ref):
    lane = lax.broadcasted_iota(jnp.int32, (w_ref.shape[0], HEAD_DIM), 1)
    low = (lane >= ROT_HALF) & (lane < ROT_DIM)
    high = (lane >= ROPE_APART) & (lane < ROPE_APART + ROT_HALF)
    gap = ROPE_APART - ROT_HALF
    for hd in range(A_QK_WIDTH // HEAD_DIM):
        lanes = slice(hd * HEAD_DIM, (hd + 1) * HEAD_DIM)
        w = w_ref[:, lanes]
        w = jnp.where(low, pltpu.roll(w, HEAD_DIM - gap, 1), jnp.where(high, pltpu.roll(w, gap, 1), w))
        qkv_ref[:, lanes] = w.astype(BF16)
    qkv_ref[:, A_QK_WIDTH:] = w_ref[:, A_QK_WIDTH:QKV_WIDTH].astype(BF16)
    rest_ref[...] = w_ref[:, QKV_WIDTH:].astype(BF16)


def _even_weights(w_in):
    n_layers = w_in.shape[0]
    spec = lambda w: pl.BlockSpec((None, WEIGHT_ROWS, w), lambda l, r: (l, r, 0))
    return pl.pallas_call(
        _even_weights_kernel,
        grid=(n_layers, D_MODEL // WEIGHT_ROWS),
        in_specs=[spec(EVEN_IN)],
        out_specs=[spec(QKV_WIDTH), spec(EVEN_REST)],
        out_shape=[jax.ShapeDtypeStruct((n_layers, D_MODEL, QKV_WIDTH), BF16),
                   jax.ShapeDtypeStruct((n_layers, D_MODEL, EVEN_REST), BF16)],
        compiler_params=_params("parallel", "parallel"),
        name="even_weights",
    )(w_in)


def _memkv_kernel(mem_ref, g_ref, w_ref, out_ref):
    for r in range(0, mem_ref.shape[0], N_MEM):
        mem_n = _rmsnorm_bf16(mem_ref[r:r + N_MEM, :], g_ref[...])
        out_ref[0, r:r + N_MEM, :] = _dot(mem_n, w_ref[0]).astype(BF16)


def _memory_kv(mem, g_mem, w_kv):
    n_layers = w_kv.shape[0]
    batch = mem.shape[0]
    rows = batch * N_MEM
    out = pl.pallas_call(
        _memkv_kernel,
        grid=(n_layers,),
        in_specs=[_const_spec((rows, D_MODEL)), _const_spec((1, D_MODEL)),
                  pl.BlockSpec((1, D_MODEL, 2 * M_WIDTH), lambda l: (l, 0, 0))],
        out_specs=pl.BlockSpec((1, rows, 2 * M_WIDTH), lambda l: (l, 0, 0)),
        out_shape=jax.ShapeDtypeStruct((n_layers, rows, 2 * M_WIDTH), BF16),
        compiler_params=_params("parallel"),
        name="memory_kv",
    )(mem.reshape(rows, D_MODEL), g_mem.reshape(1, D_MODEL), w_kv.astype(BF16))
    return out.reshape(n_layers, batch, N_MEM, 2 * M_WIDTH)


def _memory_attention(p_ref, col0, mk_ref, mv_ref):
    outs = []
    ones = jnp.ones((N_MEM, LANES), BF16)
    for h in range(MEM_HEADS):
        lanes = slice(h * HEAD_DIM, (h + 1) * HEAD_DIM)
        q = p_ref[:, col0 + h * HEAD_DIM:col0 + (h + 1) * HEAD_DIM].astype(BF16)
        sc = _dot_nt(q, mk_ref[:, lanes]) * SCALE_LOG2
        e = jnp.exp2(sc - jnp.max(sc, axis=-1, keepdims=True)).astype(BF16)
        both = _dot(e, jnp.concatenate([mv_ref[:, lanes], ones], axis=1))
        outs.append(both[:, :HEAD_DIM] * (1.0 / both[:, HEAD_DIM:]))
    return outs


def _qkv_kernel(x_ref, g_ref, w_ref, c_ref, s_ref, *refs):
    n_qk = 2 * N_GROUPS
    group_refs = refs[:N_GROUPS]
    h_ref, acc_ref, st_ref = refs[N_GROUPS:]
    heads = COL_CHUNK // HEAD_DIM
    tile_rows = x_ref.shape[0]
    _rmsnorm_rows(x_ref, g_ref, h_ref)

    def by_residue(head, targets):
        for b in range(DEINT):
            part = acc_ref[head, pl.ds(b, tile_rows // DEINT, stride=DEINT), :]
            for ref, slot, dil in targets:
                if dil == DEINT:
                    ref[0, slot, head, b] = part.astype(BF16)
                else:
                    st_ref[head, b] = part
                    for a in range(DEINT):
                        ref[0, slot, head, DEINT * a + b] = (
                            st_ref[head, b, pl.ds(a, tile_rows // dil, stride=DEINT), :].astype(BF16))

    for c in range(QKV_WIDTH // COL_CHUNK):
        res = _dot(h_ref[...], w_ref[:, c * COL_CHUNK:(c + 1) * COL_CHUNK])
        for hh in range(heads):
            acc_ref[hh] = res[:, hh * HEAD_DIM:(hh + 1) * HEAD_DIM]
        if c < n_qk:
            ref, slot, dil = group_refs[c // 2], c % 2, DIL_GROUPS[c // 2][1]
            for r in range(0, tile_rows, ROPE_BAND):
                rows = slice(r, r + ROPE_BAND)
                c_tab, s_tab = c_ref[rows, :], s_ref[rows, :]
                for hh in range(heads):
                    blk = acc_ref[hh, rows, :]
                    blk = blk * c_tab + pltpu.roll(blk, ROPE_APART, 1) * s_tab
                    if dil == 1:
                        ref[0, slot, hh, 0, rows, :] = blk.astype(BF16)
                    else:
                        acc_ref[hh, rows, :] = blk
            if dil > 1:
                for hh in range(heads):
                    by_residue(hh, [(ref, slot, dil)])
        else:
            for hh in range(heads):
                group_refs[0][0, 2, hh, 0] = acc_ref[hh].astype(BF16)
                by_residue(hh, [(group_refs[gi], 2, DIL_GROUPS[gi][1]) for gi in range(1, N_GROUPS)])


def _qkv_project(x2, g, w_qkv, layer, tabs, batch, seq):
    rows = x2.shape[0]
    tile = QKV_TILE
    tiles_per_seq = seq // tile
    row_spec = lambda w: pl.BlockSpec((tile, w), lambda i: (i, 0))
    dils = [d for _, d in DIL_GROUPS]
    out_spec = lambda d: pl.BlockSpec((1, 3, A_HEADS, d, tile // d, HEAD_DIM),
                                      lambda i: (i // tiles_per_seq, 0, 0, 0, i % tiles_per_seq, 0))
    out_shape = lambda d: jax.ShapeDtypeStruct((batch, 3, A_HEADS, d, seq // d, HEAD_DIM), BF16)
    return pl.pallas_call(
        _qkv_kernel,
        grid=(rows // tile,),
        in_specs=[row_spec(D_MODEL), _const_spec((1, D_MODEL)), _layer_spec((D_MODEL, QKV_WIDTH), layer),
                  row_spec(LANES), row_spec(LANES)],
        out_specs=[out_spec(d) for d in dils],
        out_shape=[out_shape(d) for d in dils],
        scratch_shapes=[pltpu.VMEM((tile, D_MODEL), BF16),
                        pltpu.VMEM((COL_CHUNK // HEAD_DIM, tile, HEAD_DIM), F32),
                        pltpu.VMEM((COL_CHUNK // HEAD_DIM, DEINT, tile // DEINT, HEAD_DIM), F32)],
        compiler_params=_params("parallel"),
        name="even_qkv",
    )(x2, g.reshape(1, D_MODEL), w_qkv, *tabs)


def _mixer_kernel(g0, g1, g2, o_ref, l_ref, ck0, cv0, ck1, cv1, ck2, cv2, o_acc, l_acc):
    tile = pl.program_id(2)
    carries = (ck0, cv0, ck1, cv1, ck2, cv2)
    (q0, k0, v0), (q1, k1, v1), (q2, k2, v2) = ([g.at[0, s, 0] for s in range(3)] for g in (g0, g1, g2))

    @pl.when(tile == 0)
    def _():
        for c in carries:
            c[...] = jnp.zeros_like(c)

    no_prev = jnp.where(tile > 0, 0.0, -jnp.inf).astype(F32)
    row = lax.broadcasted_iota(jnp.int32, (SPAN, SPAN), 0)
    col = lax.broadcasted_iota(jnp.int32, (SPAN, SPAN), 1)
    tri_prev = col >= row
    tri_cur = col <= row
    ones = jnp.ones((2 * SPAN, LANES), BF16)

    def attend(q, kk, vv, first):
        s = _dot_nt(q, kk) * SCALE_LOG2
        sp, sc = s[:, :SPAN], s[:, SPAN:]
        if first:
            sp = sp + no_prev
        sp = jnp.where(tri_prev, sp, -jnp.inf)
        sc = jnp.where(tri_cur, sc, -jnp.inf)
        m = jnp.max(jnp.maximum(sp, sc), axis=-1, keepdims=True)
        p = jnp.concatenate([jnp.exp2(sp - m), jnp.exp2(sc - m)], axis=1).astype(BF16)
        both = _dot(p, jnp.concatenate([vv, ones], axis=1))
        num, den = both[:, :HEAD_DIM], both[:, HEAD_DIM:]
        return num * (1.0 / den), m + jnp.log2(den)

    def put(group, rows, parts):
        o_acc[group, rows, :], l_acc[group, rows, :] = parts

    cat = lambda a, b: jnp.concatenate([a, b], axis=0)

    put(0, slice(0, SPAN), attend(q0[0, 0:SPAN, :], cat(ck0[...], k0[0, 0:SPAN, :]),
                                  cat(cv0[...], v0[0, 0:SPAN, :]), True))

    def block0(jb):
        r0 = jb * SPAN if isinstance(jb, int) else pl.multiple_of(jb * SPAN, SPAN)
        both = pl.ds(r0 - SPAN, 2 * SPAN)
        put(0, pl.ds(r0, SPAN),
            attend(q0[0, pl.ds(r0, SPAN), :], k0[0, both, :], v0[0, both, :], False))

    _unrolled_loop(1, SUPER // SPAN, UNROLL_D1, block0)
    ck0[...] = k0[0, SUPER - SPAN:SUPER, :]
    cv0[...] = v0[0, SUPER - SPAN:SUPER, :]

    dil1 = DIL_GROUPS[1][1]
    n_blk1 = SUPER // (dil1 * SPAN)

    def residue1(rho):
        for jb in range(n_blk1):
            rows = slice(jb * SPAN, (jb + 1) * SPAN)
            if jb == 0:
                kk, vv = cat(ck1[rho], k1[rho, rows, :]), cat(cv1[rho], v1[rho, rows, :])
            else:
                both = slice((jb - 1) * SPAN, (jb + 1) * SPAN)
                kk, vv = k1[rho, both, :], v1[rho, both, :]
            put(1, pl.ds(jb * dil1 * SPAN + rho, SPAN, stride=dil1),
                attend(q1[rho, rows, :], kk, vv, jb == 0))
        last = slice((n_blk1 - 1) * SPAN, n_blk1 * SPAN)
        ck1[rho] = k1[rho, last, :]
        cv1[rho] = v1[rho, last, :]

    _unrolled_loop(0, dil1, UNROLL_D4, residue1)

    dil2 = DIL_GROUPS[2][1]

    def residue2(rho):
        put(2, pl.ds(rho, SPAN, stride=dil2),
            attend(q2[rho], cat(ck2[rho], k2[rho]), cat(cv2[rho], v2[rho]), True))
        ck2[rho] = k2[rho]
        cv2[rho] = v2[rho]

    _unrolled_loop(0, dil2, UNROLL_D16, residue2)

    lane = lax.broadcasted_iota(jnp.int32, (MIX_BAND, LANES), 1)

    def emit(i, carry):
        rows = pl.ds(pl.multiple_of(i * MIX_BAND, MIX_BAND), MIX_BAND)
        packed = l_acc[N_GROUPS - 1, rows, :]
        for gi in reversed(range(N_GROUPS - 1)):
            packed = jnp.where(lane < (gi + 1) * LSE_LANES, l_acc[gi, rows, :], packed)
        l_ref[0, 0, rows, :] = packed
        for gi in range(N_GROUPS):
            o_ref[0, 0, gi, rows, :] = o_acc[gi, rows, :].astype(BF16)
        return carry

    lax.fori_loop(0, SUPER // MIX_BAND, emit, 0)


def _dilated_mixer(pieces, batch, seq):
    specs, carries = [], []
    for _, dil in DIL_GROUPS:
        specs.append(pl.BlockSpec((1, 3, 1, dil, SUPER // dil, HEAD_DIM), lambda b, h, t: (b, 0, h, 0, t, 0)))
        shape = (SPAN, HEAD_DIM) if dil == 1 else (dil, SPAN, HEAD_DIM)
        carries += [pltpu.VMEM(shape, BF16)] * 2
    return pl.pallas_call(
        _mixer_kernel,
        grid=(batch, A_HEADS, seq // SUPER),
        in_specs=specs,
        out_specs=[pl.BlockSpec((1, 1, N_GROUPS, SUPER, HEAD_DIM), lambda b, h, t: (b, h, 0, t, 0)),
                   pl.BlockSpec((1, 1, SUPER, LANES), lambda b, h, t: (b, h, t, 0))],
        out_shape=[jax.ShapeDtypeStruct((batch, A_HEADS, N_GROUPS, seq, HEAD_DIM), BF16),
                   jax.ShapeDtypeStruct((batch, A_HEADS, seq, LANES), F32)],
        scratch_shapes=carries + [pltpu.VMEM((N_GROUPS, SUPER, HEAD_DIM), F32)] * 2,
        compiler_params=_params("parallel", "parallel", "arbitrary"),
        name="dilated_mixer",
    )(*pieces)


def _even_tail_kernel(x_ref, halo_ref, g_ref, w_ref, wpool_ref, pscale_ref, mk_ref, mv_ref,
                      wout_ref, o_ref, l_ref, out_ref, h_ref, p_ref, y_ref, *, tiles_per_seq):
    seq_tile = pl.program_id(0) % tiles_per_seq
    tile_rows = x_ref.shape[0]
    _rmsnorm_rows(x_ref, g_ref, h_ref)
    _project(h_ref, w_ref, p_ref)
    z0 = B_WIDTH + M_WIDTH

    def out_project(k0):
        yk = y_ref[:, k0:k0 + COL_CHUNK]
        for c in range(0, D_MODEL, COL_CHUNK):
            cols = slice(c, c + COL_CHUNK)
            base = x_ref[:, cols] if k0 == 0 else out_ref[:, cols]
            out_ref[:, cols] = base + _dot(yk, wout_ref[k0:k0 + COL_CHUNK, cols])

    for hd in range(A_HEADS):
        for r in range(0, tile_rows, ROW_BAND):
            rows = slice(r, r + ROW_BAND)
            packed = l_ref[0, hd, rows, :]
            ls = [jnp.broadcast_to(packed[:, gi * LSE_LANES:gi * LSE_LANES + 1], (ROW_BAND, LANES))
                  for gi in range(N_GROUPS)]
            top = functools.reduce(jnp.maximum, ls)
            es = [jnp.exp2(l - top) for l in ls]
            inv = 1.0 / functools.reduce(lambda a, b: a + b, es)
            a = sum((es[gi] * inv) * o_ref[0, hd, gi, rows, :].astype(F32) for gi in range(N_GROUPS))
            za = p_ref[rows, z0 + hd * HEAD_DIM:z0 + (hd + 1) * HEAD_DIM]
            y_ref[rows, hd * HEAD_DIM:(hd + 1) * HEAD_DIM] = (a * _silu(za)).astype(BF16)
    out_project(0)

    h_halo = _rmsnorm_bf16(halo_ref[...], g_ref[...])
    xb_halo = _dot(h_halo, w_ref[:, 0:B_WIDTH]) * (seq_tile > 0).astype(F32)
    run = jnp.concatenate([xb_halo, p_ref[:, 0:B_WIDTH]], axis=0)
    t = seq_tile * tile_rows + lax.broadcasted_iota(jnp.int32, (tile_rows, POOL_CH), 0)
    shift = 1
    for gi, w in enumerate(POOL_SIZES):
        while shift < w:
            run = run + pltpu.roll(run, shift, 0)
            shift *= 2
        lanes = slice(gi * POOL_CH, (gi + 1) * POOL_CH)
        cnt = jnp.minimum(t + 1, w).astype(F32)
        pooled = run[POOL_HALO:, lanes] / cnt - p_ref[:, lanes]
        yb = _dot(pooled.astype(BF16), wpool_ref[gi]) * pscale_ref[:, lanes]
        zb = p_ref[:, z0 + A_WIDTH + gi * POOL_CH:z0 + A_WIDTH + (gi + 1) * POOL_CH]
        y_ref[:, A_WIDTH + gi * POOL_CH:A_WIDTH + (gi + 1) * POOL_CH] = (yb * _silu(zb)).astype(BF16)
    out_project(A_WIDTH)

    m_out = _memory_attention(p_ref, B_WIDTH, mk_ref.at[0, 0], mv_ref.at[0, 0])
    for hd in range(MEM_HEADS):
        c0 = z0 + A_WIDTH + B_WIDTH + hd * HEAD_DIM
        y0 = A_WIDTH + B_WIDTH + hd * HEAD_DIM
        y_ref[:, y0:y0 + HEAD_DIM] = (m_out[hd] * _silu(p_ref[:, c0:c0 + HEAD_DIM])).astype(BF16)
    out_project(A_WIDTH + B_WIDTH)


def _even_tail(x2, g, w_rest, w_pool, pool_scale, memkv, layer, w_out, a_parts, batch, seq):
    rows = x2.shape[0]
    tile_rows = TAIL_TILE
    tiles_per_seq = seq // tile_rows
    halo_blocks = tile_rows // POOL_HALO
    row_spec = lambda w: pl.BlockSpec((tile_rows, w), lambda i: (i, 0))
    halo_spec = pl.BlockSpec((POOL_HALO, D_MODEL), lambda i: (jnp.maximum(i * halo_blocks - 1, 0), 0))
    mk_spec = pl.BlockSpec((1, 1, N_MEM, M_WIDTH), lambda i: (layer, i // tiles_per_seq, 0, 0))
    mv_spec = pl.BlockSpec((1, 1, N_MEM, M_WIDTH), lambda i: (layer, i // tiles_per_seq, 0, 1))
    return pl.pallas_call(
        functools.partial(_even_tail_kernel, tiles_per_seq=tiles_per_seq),
        grid=(rows // tile_rows,),
        in_specs=[row_spec(D_MODEL), halo_spec, _const_spec((1, D_MODEL)),
                  _layer_spec((D_MODEL, EVEN_REST), layer),
                  _layer_spec((len(POOL_SIZES), POOL_CH, POOL_CH), layer),
                  _const_spec((1, B_WIDTH)), mk_spec, mv_spec, _layer_spec((EVEN_MIX, D_MODEL), layer),
                  pl.BlockSpec((1, A_HEADS, N_GROUPS, tile_rows, HEAD_DIM),
                               lambda i: (i // tiles_per_seq, 0, 0, i % tiles_per_seq, 0)),
                  pl.BlockSpec((1, A_HEADS, tile_rows, LANES),
                               lambda i: (i // tiles_per_seq, 0, i % tiles_per_seq, 0))],
        out_specs=row_spec(D_MODEL),
        out_shape=jax.ShapeDtypeStruct((rows, D_MODEL), F32),
        scratch_shapes=[pltpu.VMEM((tile_rows, D_MODEL), BF16), pltpu.VMEM((tile_rows, EVEN_REST), F32),
                        pltpu.VMEM((tile_rows, EVEN_MIX), BF16)],
        compiler_params=_params("parallel"),
        name="even_tail",
    )(x2, x2, g.reshape(1, D_MODEL), w_rest, w_pool, pool_scale.reshape(1, B_WIDTH),
      memkv, memkv, w_out, *a_parts)


def _even_layer(x2, tabs, memkv, layer, g, w_qkv, w_rest, w_pool, pool_scale, w_out, batch, seq):
    assert all(d in (1, DEINT, DEINT * DEINT) for _, d in DIL_GROUPS)
    pieces = _qkv_project(x2, g, w_qkv, layer, tabs, batch, seq)
    a_parts = _dilated_mixer(pieces, batch, seq)
    return _even_tail(x2, g, w_rest, w_pool, pool_scale, memkv, layer, w_out, a_parts, batch, seq)


def _odd_kernel(x_ref, g_ref, w_ref, lng_ref, lnb_ref, ws_ref, bs_ref, mk_ref, mv_ref, wout_ref,
                fg_ref, out_ref, h_ref, p_ref, vn_ref, y_ref, *, final_norm):
    tile_rows = x_ref.shape[0]
    _rmsnorm_rows(x_ref, g_ref, h_ref)
    _project(h_ref, w_ref, p_ref)
    z0 = 2 * C_WIDTH + M_WIDTH

    for r in range(0, tile_rows, ROW_BAND):
        v = p_ref[r:r + ROW_BAND, C_WIDTH:2 * C_WIDTH]
        vc = v - jnp.mean(v, axis=-1, keepdims=True)
        var = jnp.mean(vc * vc, axis=-1, keepdims=True)
        vn_ref[r:r + ROW_BAND, :] = ((vc * lax.rsqrt(var + EPS)) * lng_ref[...] + lnb_ref[...]).astype(BF16)

    res_ref, res0 = (p_ref, C_WIDTH) if final_norm else (out_ref, 0)

    def out_project(k0):
        yk = y_ref[:, k0:k0 + COL_CHUNK]
        for c in range(0, D_MODEL, COL_CHUNK):
            dst = slice(res0 + c, res0 + c + COL_CHUNK)
            base = x_ref[:, c:c + COL_CHUNK] if k0 == 0 else res_ref[:, dst]
            res_ref[:, dst] = base + _dot(yk, wout_ref[k0:k0 + COL_CHUNK, c:c + COL_CHUNK])

    tri_r = lax.broadcasted_iota(jnp.int32, (CHUNK, CHUNK), 0)
    tri_c = lax.broadcasted_iota(jnp.int32, (CHUNK, CHUNK), 1)
    causal = tri_c <= tri_r
    for gi in range(C_GROUPS):
        lanes = slice(gi * C_CH, (gi + 1) * C_CH)
        ws = jnp.where(causal, ws_ref[gi], jnp.zeros((CHUNK, CHUNK), BF16))
        bias = bs_ref[:, gi:gi + 1]
        n_chunks = tile_rows // CHUNK
        vn = jnp.concatenate([vn_ref[n * CHUNK:(n + 1) * CHUNK, lanes] for n in range(n_chunks)], axis=1)
        mixed_all = _dot(ws, vn)
        for n in range(n_chunks):
            rows = slice(n * CHUNK, (n + 1) * CHUNK)
            mixed = mixed_all[:, n * C_CH:(n + 1) * C_CH] + bias
            gate = _silu(p_ref[rows, z0 + gi * C_CH:z0 + (gi + 1) * C_CH])
            y_ref[rows, lanes] = (p_ref[rows, lanes] * mixed * gate).astype(BF16)
        if (gi + 1) * C_CH % COL_CHUNK == 0:
            out_project((gi + 1) * C_CH - COL_CHUNK)

    m_out = _memory_attention(p_ref, 2 * C_WIDTH, mk_ref.at[0, 0], mv_ref.at[0, 0])
    for hd in range(MEM_HEADS):
        c0 = z0 + C_WIDTH + hd * HEAD_DIM
        y0 = C_WIDTH + hd * HEAD_DIM
        y_ref[:, y0:y0 + HEAD_DIM] = (m_out[hd] * _silu(p_ref[:, c0:c0 + HEAD_DIM])).astype(BF16)
    out_project(C_WIDTH)

    if final_norm:
        for r in range(0, tile_rows, ROW_BAND):
            res = p_ref[r:r + ROW_BAND, res0:res0 + D_MODEL]
            ms = jnp.mean(res * res, axis=-1, keepdims=True)
            out_ref[r:r + ROW_BAND, :] = (res * lax.rsqrt(ms + EPS)) * fg_ref[...]


def _odd_layer(x2, memkv, layer, g, w_in, ln_g, ln_b, w_s, b_s, w_out, final_g, final_norm, seq):
    rows = x2.shape[0]
    tile_rows = ODD_TILE
    tiles_per_seq = seq // tile_rows
    row_spec = pl.BlockSpec((tile_rows, D_MODEL), lambda i: (i, 0))
    mk_spec = pl.BlockSpec((1, 1, N_MEM, M_WIDTH), lambda i: (layer, i // tiles_per_seq, 0, 0))
    mv_spec = pl.BlockSpec((1, 1, N_MEM, M_WIDTH), lambda i: (layer, i // tiles_per_seq, 0, 1))
    return pl.pallas_call(
        functools.partial(_odd_kernel, final_norm=final_norm),
        grid=(rows // tile_rows,),
        in_specs=[row_spec, _const_spec((1, D_MODEL)), _layer_spec((D_MODEL, ODD_IN), layer),
                  _const_spec((1, C_WIDTH)), _const_spec((1, C_WIDTH)),
                  _layer_spec((C_GROUPS, CHUNK, CHUNK), layer), _const_spec((CHUNK, C_GROUPS)),
                  mk_spec, mv_spec, _layer_spec((ODD_MIX, D_MODEL), layer), _const_spec((1, D_MODEL))],
        out_specs=row_spec,
        out_shape=jax.ShapeDtypeStruct((rows, D_MODEL), F32),
        scratch_shapes=[pltpu.VMEM((tile_rows, D_MODEL), BF16), pltpu.VMEM((tile_rows, ODD_IN), F32),
                        pltpu.VMEM((tile_rows, C_WIDTH), BF16), pltpu.VMEM((tile_rows, ODD_MIX), BF16)],
        compiler_params=_params("parallel"),
        name="odd_layer",
    )(x2, g.reshape(1, D_MODEL), w_in, ln_g.reshape(1, C_WIDTH), ln_b.reshape(1, C_WIDTH),
      w_s, b_s.T, memkv, memkv, w_out, final_g.reshape(1, D_MODEL))


def kernel(x, mem, positions, g_mem, even_norm_g, even_w_in, even_w_pool, even_pool_scale,
           even_w_mem_kv, even_w_out, odd_norm_g, odd_w_in, odd_ln_g, odd_ln_b, odd_w_s,
           odd_b_s, odd_w_mem_kv, odd_w_out, final_norm_g):
    batch, seq, _ = x.shape
    depth = even_norm_g.shape[0] + odd_norm_g.shape[0]
    assert all(seq % t == 0 for t in (QKV_TILE, TAIL_TILE, ODD_TILE, SUPER))

    tabs = _rope_tables(positions)
    memkv_even = _memory_kv(mem, g_mem, even_w_mem_kv)
    memkv_odd = _memory_kv(mem, g_mem, odd_w_mem_kv)
    even_w_qkv, even_w_rest = _even_weights(even_w_in)
    even_w_pool, even_w_out, odd_w_in, odd_w_s, odd_w_out = (
        w.astype(BF16) for w in (even_w_pool, even_w_out, odd_w_in, odd_w_s, odd_w_out))

    x2 = x.reshape(batch * seq, D_MODEL)
    for layer in range(depth):
        i = layer // 2
        if layer % 2 == 0:
            x2 = _even_layer(x2, tabs, memkv_even, i, even_norm_g[i], even_w_qkv, even_w_rest, even_w_pool,
                             even_pool_scale[i], even_w_out, batch, seq)
        else:
            x2 = _odd_layer(x2, memkv_odd, i, odd_norm_g[i], odd_w_in, odd_ln_g[i], odd_ln_b[i],
                            odd_w_s, odd_b_s[i], odd_w_out, final_norm_g,
                            layer == depth - 1, seq)
    return x2.reshape(batch, seq, D_MODEL)
```

```python
import functools
import math

import jax
import jax.numpy as jnp
from jax import lax
from jax.experimental import pallas as pl
from jax.experimental.pallas import tpu as pltpu

D_MODEL = 1024
HEAD_DIM = 128
EPS = 1e-6
ROPE_THETA = 500000.0
ROT_DIM = HEAD_DIM // 4
ROT_HALF = ROT_DIM // 2
N_MEM = 256
MEM_HEADS = 4
DIL_GROUPS = ((128, 1), (512, 4), (2048, 16))
A_HEADS = 4
POOL_SIZES = (2, 4, 8, 16)
POOL_CH = 128
CHUNK = 128
C_GROUPS = 8
C_CH = 128

A_WIDTH = A_HEADS * HEAD_DIM
B_WIDTH = len(POOL_SIZES) * POOL_CH
C_WIDTH = C_GROUPS * C_CH
M_WIDTH = MEM_HEADS * HEAD_DIM
EVEN_MIX = A_WIDTH + B_WIDTH + M_WIDTH
ODD_MIX = C_WIDTH + M_WIDTH
N_GROUPS = len(DIL_GROUPS)
A_QK_WIDTH = 2 * N_GROUPS * A_WIDTH
QKV_WIDTH = A_QK_WIDTH + A_WIDTH
EVEN_IN = QKV_WIDTH + B_WIDTH + M_WIDTH + EVEN_MIX
EVEN_REST = EVEN_IN - QKV_WIDTH
ODD_IN = 2 * C_WIDTH + M_WIDTH + ODD_MIX
SPAN = 128
SCALE = HEAD_DIM ** -0.5
SCALE_LOG2 = SCALE * math.log2(math.e)
POOL_HALO = max(POOL_SIZES)
SUPER = max(d for _, d in DIL_GROUPS) * SPAN
MIX_BAND = 256
UNROLL_D4 = 4
UNROLL_D16 = 16

LANES = 128
QKV_TILE = 1024
TAIL_TILE = 1024
ODD_TILE = 1024
ROW_BAND = 128
ROPE_BAND = 64
ROPE_APART = 64
DEINT = 4
WEIGHT_ROWS = 256
COL_CHUNK = 512
VMEM_LIMIT = 56 * 1024 * 1024

F32 = jnp.float32
BF16 = jnp.bfloat16


def _const_spec(shape):
    nd = len(shape)
    return pl.BlockSpec(shape, lambda *_: (0,) * nd, pipeline_mode=pl.Buffered(1))


def _layer_spec(shape, layer):
    nd = len(shape)
    return pl.BlockSpec((None,) + tuple(shape), lambda *_: (layer,) + (0,) * nd,
                        pipeline_mode=pl.Buffered(1))


def _params(*sem):
    return pltpu.CompilerParams(dimension_semantics=sem, vmem_limit_bytes=VMEM_LIMIT)


def _rmsnorm_bf16(x, g):
    ms = jnp.mean(x * x, axis=-1, keepdims=True)
    return ((x * lax.rsqrt(ms + EPS)) * g).astype(BF16)


def _rmsnorm_rows(x_ref, g_ref, h_ref):
    g = g_ref[...]
    for r in range(0, x_ref.shape[0], ROW_BAND):
        h_ref[r:r + ROW_BAND, :] = _rmsnorm_bf16(x_ref[r:r + ROW_BAND, :], g)


def _project(h_ref, w_ref, p_ref, col0=0):
    for c in range(0, p_ref.shape[1], COL_CHUNK):
        p_ref[:, c:c + COL_CHUNK] = _dot(h_ref[...], w_ref[:, col0 + c:col0 + c + COL_CHUNK])


def _unrolled_loop(lo, hi, unroll, fn):
    n = hi - lo
    assert n % unroll == 0
    if unroll == n:
        for i in range(lo, hi):
            fn(i)
        return

    def trip(t, carry):
        for s in range(unroll):
            fn(lo + t * unroll + s)
        return carry

    lax.fori_loop(0, n // unroll, trip, 0)


def _dot(a, b):
    return jnp.dot(a, b, preferred_element_type=F32)


def _dot_nt(a, b):
    return lax.dot_general(a, b, (((1,), (1,)), ((), ())), preferred_element_type=F32)


def _silu(z):
    return z * (1.0 / (1.0 + jnp.exp(-z)))


ROPE_PACK = ROPE_APART // ROT_HALF
ROPE_ROWS = 256


def _rope_kernel(pos_ref, inv_ref, sign_ref, c_ref, s_ref):
    ang = pos_ref[...].astype(F32) * inv_ref[...]
    cos = jnp.cos(ang)
    sin = jnp.sin(ang) * sign_ref[...]
    lane = lax.broadcasted_iota(jnp.int32, cos.shape, 1)
    rotary = (lane % ROPE_APART) < ROT_HALF
    for j in range(ROPE_PACK):
        shift = (LANES - ROT_HALF * j) % LANES
        cj = cos if shift == 0 else pltpu.roll(cos, shift, 1)
        sj = sin if shift == 0 else pltpu.roll(sin, shift, 1)
        rows = pl.ds(j, ROPE_ROWS, stride=ROPE_PACK)
        c_ref[rows, :] = jnp.where(rotary, cj, 1.0)
        s_ref[rows, :] = jnp.where(rotary, sj, 0.0)


def _rope_tables(positions):
    n = positions.size
    inv = ROPE_THETA ** (-jnp.arange(0, ROT_DIM, 2, dtype=F32) / ROT_DIM)
    slots = jnp.repeat(positions.reshape(n // ROPE_PACK, ROPE_PACK), ROT_HALF, axis=1)
    pos_c = jnp.tile(slots, (1, LANES // ROPE_APART))
    inv_c = jnp.tile(inv, LANES // ROT_HALF).reshape(1, LANES)
    sign = jnp.where(jnp.arange(LANES) < ROPE_APART, -1.0, 1.0).astype(F32).reshape(1, LANES)
    lane_spec = _const_spec((1, LANES))
    out_spec = pl.BlockSpec((ROPE_ROWS * ROPE_PACK, LANES), lambda i: (i, 0))
    return pl.pallas_call(
        _rope_kernel,
        grid=(n // (ROPE_ROWS * ROPE_PACK),),
        in_specs=[pl.BlockSpec((ROPE_ROWS, LANES), lambda i: (i, 0)), lane_spec, lane_spec],
        out_specs=[out_spec, out_spec],
        out_shape=(jax.ShapeDtypeStruct((n, LANES), F32),) * 2,
        compiler_params=_params("parallel"),
        name="rope_tables",
    )(pos_c, inv_c, sign)


def _even_weights_kernel(w_ref, qkv_ref, rest_ref):
    lane = lax.broadcasted_iota(jnp.int32, (w_ref.shape[0], HEAD_DIM), 1)
    low = (lane >= ROT_HALF) & (lane < ROT_DIM)
    high = (lane >= ROPE_APART) & (lane < ROPE_APART + ROT_HALF)
    gap = ROPE_APART - ROT_HALF
    for hd in range(A_QK_WIDTH // HEAD_DIM):
        lanes = slice(hd * HEAD_DIM, (hd + 1) * HEAD_DIM)
        w = w_ref[:, lanes]
        w = jnp.where(low, pltpu.roll(w, HEAD_DIM - gap, 1), jnp.where(high, pltpu.roll(w, gap, 1), w))
        qkv_ref[:, lanes] = w.astype(BF16)
    qkv_ref[:, A_QK_WIDTH:] = w_ref[:, A_QK_WIDTH:QKV_WIDTH].astype(BF16)
    rest_ref[...] = w_ref[:, QKV_WIDTH:].astype(BF16)


def _even_weights(w_in):
    n_layers = w_in.shape[0]
    spec = lambda w: pl.BlockSpec((None, WEIGHT_ROWS, w), lambda l, r: (l, r, 0))
    return pl.pallas_call(
        _even_weights_kernel,
        grid=(n_layers, D_MODEL // WEIGHT_ROWS),
        in_specs=[spec(EVEN_IN)],
        out_specs=[spec(QKV_WIDTH), spec(EVEN_REST)],
        out_shape=[jax.ShapeDtypeStruct((n_layers, D_MODEL, QKV_WIDTH), BF16),
                   jax.ShapeDtypeStruct((n_layers, D_MODEL, EVEN_REST), BF16)],
        compiler_params=_params("parallel", "parallel"),
        name="even_weights",
    )(w_in)


def _memkv_kernel(mem_ref, g_ref, w_ref, out_ref):
    for r in range(0, mem_ref.shape[0], N_MEM):
        mem_n = _rmsnorm_bf16(mem_ref[r:r + N_MEM, :], g_ref[...])
        out_ref[0, r:r + N_MEM, :] = _dot(mem_n, w_ref[0]).astype(BF16)


def _memory_kv(mem, g_mem, w_kv):
    n_layers = w_kv.shape[0]
    batch = mem.shape[0]
    rows = batch * N_MEM
    out = pl.pallas_call(
        _memkv_kernel,
        grid=(n_layers,),
        in_specs=[_const_spec((rows, D_MODEL)), _const_spec((1, D_MODEL)),
                  pl.BlockSpec((1, D_MODEL, 2 * M_WIDTH), lambda l: (l, 0, 0))],
        out_specs=pl.BlockSpec((1, rows, 2 * M_WIDTH), lambda l: (l, 0, 0)),
        out_shape=jax.ShapeDtypeStruct((n_layers, rows, 2 * M_WIDTH), BF16),
        compiler_params=_params("parallel"),
        name="memory_kv",
    )(mem.reshape(rows, D_MODEL), g_mem.reshape(1, D_MODEL), w_kv.astype(BF16))
    return out.reshape(n_layers, batch, N_MEM, 2 * M_WIDTH)


def _memory_attention(p_ref, col0, mk_ref, mv_ref):
    outs = []
    ones = jnp.ones((N_MEM, LANES), BF16)
    for h in range(MEM_HEADS):
        lanes = slice(h * HEAD_DIM, (h + 1) * HEAD_DIM)
        q = p_ref[:, col0 + h * HEAD_DIM:col0 + (h + 1) * HEAD_DIM].astype(BF16)
        sc = _dot_nt(q, mk_ref[:, lanes]) * SCALE_LOG2
        e = jnp.exp2(sc - jnp.max(sc, axis=-1, keepdims=True)).astype(BF16)
        both = _dot(e, jnp.concatenate([mv_ref[:, lanes], ones], axis=1))
        outs.append(both[:, :HEAD_DIM] * (1.0 / both[:, HEAD_DIM:]))
    return outs


def _qkv_kernel(x_ref, g_ref, w_ref, c_ref, s_ref, *refs):
    n_qk = 2 * N_GROUPS
    group_refs = refs[:N_GROUPS]
    h_ref, acc_ref, st_ref = refs[N_GROUPS:]
    heads = COL_CHUNK // HEAD_DIM
    tile_rows = x_ref.shape[0]
    _rmsnorm_rows(x_ref, g_ref, h_ref)

    def by_residue(head, targets):
        for b in range(DEINT):
            part = acc_ref[head, pl.ds(b, tile_rows // DEINT, stride=DEINT), :]
            for ref, slot, dil in targets:
                if dil == DEINT:
                    ref[0, slot, head, b] = part.astype(BF16)
                else:
                    st_ref[head, b] = part
                    for a in range(DEINT):
                        ref[0, slot, head, DEINT * a + b] = (
                            st_ref[head, b, pl.ds(a, tile_rows // dil, stride=DEINT), :].astype(BF16))

    for c in range(QKV_WIDTH // COL_CHUNK):
        res = _dot(h_ref[...], w_ref[:, c * COL_CHUNK:(c + 1) * COL_CHUNK])
        for hh in range(heads):
            acc_ref[hh] = res[:, hh * HEAD_DIM:(hh + 1) * HEAD_DIM]
        if c < n_qk:
            ref, slot, dil = group_refs[c // 2], c % 2, DIL_GROUPS[c // 2][1]
            for r in range(0, tile_rows, ROPE_BAND):
                rows = slice(r, r + ROPE_BAND)
                c_tab, s_tab = c_ref[rows, :], s_ref[rows, :]
                for hh in range(heads):
                    blk = acc_ref[hh, rows, :]
                    blk = blk * c_tab + pltpu.roll(blk, ROPE_APART, 1) * s_tab
                    if dil == 1:
                        ref[0, slot, hh, 0, rows, :] = blk.astype(BF16)
                    else:
                        acc_ref[hh, rows, :] = blk
            if dil > 1:
                for hh in range(heads):
                    by_residue(hh, [(ref, slot, dil)])
        else:
            for hh in range(heads):
                group_refs[0][0, 2, hh, 0] = acc_ref[hh].astype(BF16)
                by_residue(hh, [(group_refs[gi], 2, DIL_GROUPS[gi][1]) for gi in range(1, N_GROUPS)])


def _qkv_project(x2, g, w_qkv, layer, tabs, batch, seq):
    rows = x2.shape[0]
    tile = QKV_TILE
    tiles_per_seq = seq // tile
    row_spec = lambda w: pl.BlockSpec((tile, w), lambda i: (i, 0))
    dils = [d for _, d in DIL_GROUPS]
    out_spec = lambda d: pl.BlockSpec((1, 3, A_HEADS, d, tile // d, HEAD_DIM),
                                      lambda i: (i // tiles_per_seq, 0, 0, 0, i % tiles_per_seq, 0))
    out_shape = lambda d: jax.ShapeDtypeStruct((batch, 3, A_HEADS, d, seq // d, HEAD_DIM), BF16)
    return pl.pallas_call(
        _qkv_kernel,
        grid=(rows // tile,),
        in_specs=[row_spec(D_MODEL), _const_spec((1, D_MODEL)), _layer_spec((D_MODEL, QKV_WIDTH), layer),
                  row_spec(LANES), row_spec(LANES)],
        out_specs=[out_spec(d) for d in dils],
        out_shape=[out_shape(d) for d in dils],
        scratch_shapes=[pltpu.VMEM((tile, D_MODEL), BF16),
                        pltpu.VMEM((COL_CHUNK // HEAD_DIM, tile, HEAD_DIM), F32),
                        pltpu.VMEM((COL_CHUNK // HEAD_DIM, DEINT, tile // DEINT, HEAD_DIM), F32)],
        compiler_params=_params("parallel"),
        name="even_qkv",
    )(x2, g.reshape(1, D_MODEL), w_qkv, *tabs)


def _mixer_kernel(g0, g1, g2, out_ref, ck0, cv0, ck1, cv1, ck2, cv2, o_acc, l_acc):
    tile = pl.program_id(2)
    carries = (ck0, cv0, ck1, cv1, ck2, cv2)
    (q0, k0, v0), (q1, k1, v1), (q2, k2, v2) = ([g.at[0, s, 0] for s in range(3)] for g in (g0, g1, g2))

    @pl.when(tile == 0)
    def _():
        for c in carries:
            c[...] = jnp.zeros_like(c)

    no_prev = jnp.where(tile > 0, 0.0, -jnp.inf).astype(F32)
    row = lax.broadcasted_iota(jnp.int32, (SPAN, SPAN), 0)
    col = lax.broadcasted_iota(jnp.int32, (SPAN, SPAN), 1)
    tri_prev = col >= row
    tri_cur = col <= row
    ones = jnp.ones((2 * SPAN, LANES), BF16)

    def attend(q, kk, vv, first):
        s = _dot_nt(q, kk) * SCALE_LOG2
        sp, sc = s[:, :SPAN], s[:, SPAN:]
        if first:
            sp = sp + no_prev
        sp = jnp.where(tri_prev, sp, -jnp.inf)
        sc = jnp.where(tri_cur, sc, -jnp.inf)
        m = jnp.max(jnp.maximum(sp, sc), axis=-1, keepdims=True)
        p = jnp.concatenate([jnp.exp2(sp - m), jnp.exp2(sc - m)], axis=1).astype(BF16)
        both = _dot(p, jnp.concatenate([vv, ones], axis=1))
        num, den = both[:, :HEAD_DIM], both[:, HEAD_DIM:]
        return num * (1.0 / den), m + jnp.log2(den)

    def put(group, rows, parts):
        o_acc[group, rows, :], l_acc[group, rows, :] = parts

    cat = lambda a, b: jnp.concatenate([a, b], axis=0)

    dil1 = DIL_GROUPS[1][1]
    n_blk1 = SUPER // (dil1 * SPAN)

    def residue1(rho):
        for jb in range(n_blk1):
            rows = slice(jb * SPAN, (jb + 1) * SPAN)
            if jb == 0:
                kk, vv = cat(ck1[rho], k1[rho, rows, :]), cat(cv1[rho], v1[rho, rows, :])
            else:
                both = slice((jb - 1) * SPAN, (jb + 1) * SPAN)
                kk, vv = k1[rho, both, :], v1[rho, both, :]
            put(1, pl.ds(jb * dil1 * SPAN + rho, SPAN, stride=dil1),
                attend(q1[rho, rows, :], kk, vv, jb == 0))
        last = slice((n_blk1 - 1) * SPAN, n_blk1 * SPAN)
        ck1[rho] = k1[rho, last, :]
        cv1[rho] = v1[rho, last, :]

    _unrolled_loop(0, dil1, UNROLL_D4, residue1)

    dil2 = DIL_GROUPS[2][1]

    def residue2(rho):
        put(2, pl.ds(rho, SPAN, stride=dil2),
            attend(q2[rho], cat(ck2[rho], k2[rho]), cat(cv2[rho], v2[rho]), True))
        ck2[rho] = k2[rho]
        cv2[rho] = v2[rho]

    _unrolled_loop(0, dil2, UNROLL_D16, residue2)

    def mix(i):
        rows = slice(i * MIX_BAND, (i + 1) * MIX_BAND)
        ls = [l_acc[gi, rows, :] for gi in range(N_GROUPS)]
        lmax = jnp.maximum(jnp.maximum(ls[0], ls[1]), ls[2])
        es = [jnp.exp2(l - lmax) for l in ls]
        acc = es[0] * o_acc[0, rows, :] + es[1] * o_acc[1, rows, :] + es[2] * o_acc[2, rows, :]
        out_ref[0, 0, rows, :] = (acc * (1.0 / (es[0] + es[1] + es[2]))).astype(BF16)

    blocks_per_band = MIX_BAND // SPAN
    put(0, slice(0, SPAN), attend(q0[0, 0:SPAN, :], cat(ck0[...], k0[0, 0:SPAN, :]),
                                  cat(cv0[...], v0[0, 0:SPAN, :]), True))
    for jb in range(1, SUPER // SPAN):
        both = slice((jb - 1) * SPAN, (jb + 1) * SPAN)
        put(0, slice(jb * SPAN, (jb + 1) * SPAN),
            attend(q0[0, jb * SPAN:(jb + 1) * SPAN, :], k0[0, both, :], v0[0, both, :], False))
        if (jb + 1) % blocks_per_band == 0:
            mix(jb // blocks_per_band)
    ck0[...] = k0[0, SUPER - SPAN:SUPER, :]
    cv0[...] = v0[0, SUPER - SPAN:SUPER, :]


def _dilated_mixer(pieces, batch, seq):
    specs, carries = [], []
    for _, dil in DIL_GROUPS:
        specs.append(pl.BlockSpec((1, 3, 1, dil, SUPER // dil, HEAD_DIM), lambda b, h, t: (b, 0, h, 0, t, 0)))
        shape = (SPAN, HEAD_DIM) if dil == 1 else (dil, SPAN, HEAD_DIM)
        carries += [pltpu.VMEM(shape, BF16)] * 2
    return pl.pallas_call(
        _mixer_kernel,
        grid=(batch, A_HEADS, seq // SUPER),
        in_specs=specs,
        out_specs=pl.BlockSpec((1, 1, SUPER, HEAD_DIM), lambda b, h, t: (b, h, t, 0)),
        out_shape=jax.ShapeDtypeStruct((batch, A_HEADS, seq, HEAD_DIM), BF16),
        scratch_shapes=carries + [pltpu.VMEM((N_GROUPS, SUPER, HEAD_DIM), F32)] * 2,
        compiler_params=_params("parallel", "parallel", "arbitrary"),
        name="dilated_mixer",
    )(*pieces)


def _even_tail_kernel(x_ref, halo_ref, g_ref, w_ref, wpool_ref, pscale_ref, mk_ref, mv_ref,
                      wout_ref, a_ref, out_ref, h_ref, p_ref, y_ref, *, tiles_per_seq):
    seq_tile = pl.program_id(0) % tiles_per_seq
    tile_rows = x_ref.shape[0]
    _rmsnorm_rows(x_ref, g_ref, h_ref)
    _project(h_ref, w_ref, p_ref)
    z0 = B_WIDTH + M_WIDTH

    def out_project(k0):
        yk = y_ref[:, k0:k0 + COL_CHUNK]
        for c in range(0, D_MODEL, COL_CHUNK):
            cols = slice(c, c + COL_CHUNK)
            base = x_ref[:, cols] if k0 == 0 else out_ref[:, cols]
            out_ref[:, cols] = base + _dot(yk, wout_ref[k0:k0 + COL_CHUNK, cols])

    for hd in range(A_HEADS):
        lanes = slice(hd * HEAD_DIM, (hd + 1) * HEAD_DIM)
        za = p_ref[:, z0 + hd * HEAD_DIM:z0 + (hd + 1) * HEAD_DIM]
        y_ref[:, lanes] = (a_ref[0, hd].astype(F32) * _silu(za)).astype(BF16)
    out_project(0)

    h_halo = _rmsnorm_bf16(halo_ref[...], g_ref[...])
    xb_halo = _dot(h_halo, w_ref[:, 0:B_WIDTH]) * (seq_tile > 0).astype(F32)
    run = jnp.concatenate([xb_halo, p_ref[:, 0:B_WIDTH]], axis=0)
    t = seq_tile * tile_rows + lax.broadcasted_iota(jnp.int32, (tile_rows, POOL_CH), 0)
    shift = 1
    for gi, w in enumerate(POOL_SIZES):
        while shift < w:
            run = run + pltpu.roll(run, shift, 0)
            shift *= 2
        lanes = slice(gi * POOL_CH, (gi + 1) * POOL_CH)
        cnt = jnp.minimum(t + 1, w).astype(F32)
        pooled = run[POOL_HALO:, lanes] / cnt - p_ref[:, lanes]
        yb = _dot(pooled.astype(BF16), wpool_ref[gi]) * pscale_ref[:, lanes]
        zb = p_ref[:, z0 + A_WIDTH + gi * POOL_CH:z0 + A_WIDTH + (gi + 1) * POOL_CH]
        y_ref[:, A_WIDTH + gi * POOL_CH:A_WIDTH + (gi + 1) * POOL_CH] = (yb * _silu(zb)).astype(BF16)
    out_project(A_WIDTH)

    m_out = _memory_attention(p_ref, B_WIDTH, mk_ref.at[0, 0], mv_ref.at[0, 0])
    for hd in range(MEM_HEADS):
        c0 = z0 + A_WIDTH + B_WIDTH + hd * HEAD_DIM
        y0 = A_WIDTH + B_WIDTH + hd * HEAD_DIM
        y_ref[:, y0:y0 + HEAD_DIM] = (m_out[hd] * _silu(p_ref[:, c0:c0 + HEAD_DIM])).astype(BF16)
    out_project(A_WIDTH + B_WIDTH)


def _even_tail(x2, g, w_rest, w_pool, pool_scale, memkv, layer, w_out, a_out, batch, seq):
    rows = x2.shape[0]
    tile_rows = TAIL_TILE
    tiles_per_seq = seq // tile_rows
    halo_blocks = tile_rows // POOL_HALO
    row_spec = lambda w: pl.BlockSpec((tile_rows, w), lambda i: (i, 0))
    halo_spec = pl.BlockSpec((POOL_HALO, D_MODEL), lambda i: (jnp.maximum(i * halo_blocks - 1, 0), 0))
    mk_spec = pl.BlockSpec((1, 1, N_MEM, M_WIDTH), lambda i: (layer, i // tiles_per_seq, 0, 0))
    mv_spec = pl.BlockSpec((1, 1, N_MEM, M_WIDTH), lambda i: (layer, i // tiles_per_seq, 0, 1))
    return pl.pallas_call(
        functools.partial(_even_tail_kernel, tiles_per_seq=tiles_per_seq),
        grid=(rows // tile_rows,),
        in_specs=[row_spec(D_MODEL), halo_spec, _const_spec((1, D_MODEL)),
                  _layer_spec((D_MODEL, EVEN_REST), layer),
                  _layer_spec((len(POOL_SIZES), POOL_CH, POOL_CH), layer),
                  _const_spec((1, B_WIDTH)), mk_spec, mv_spec, _layer_spec((EVEN_MIX, D_MODEL), layer),
                  pl.BlockSpec((1, A_HEADS, tile_rows, HEAD_DIM),
                               lambda i: (i // tiles_per_seq, 0, i % tiles_per_seq, 0))],
        out_specs=row_spec(D_MODEL),
        out_shape=jax.ShapeDtypeStruct((rows, D_MODEL), F32),
        scratch_shapes=[pltpu.VMEM((tile_rows, D_MODEL), BF16), pltpu.VMEM((tile_rows, EVEN_REST), F32),
                        pltpu.VMEM((tile_rows, EVEN_MIX), BF16)],
        compiler_params=_params("parallel"),
        name="even_tail",
    )(x2, x2, g.reshape(1, D_MODEL), w_rest, w_pool, pool_scale.reshape(1, B_WIDTH),
      memkv, memkv, w_out, a_out)


def _even_layer(x2, tabs, memkv, layer, g, w_qkv, w_rest, w_pool, pool_scale, w_out, batch, seq):
    assert all(d in (1, DEINT, DEINT * DEINT) for _, d in DIL_GROUPS)
    pieces = _qkv_project(x2, g, w_qkv, layer, tabs, batch, seq)
    a_out = _dilated_mixer(pieces, batch, seq)
    return _even_tail(x2, g, w_rest, w_pool, pool_scale, memkv, layer, w_out, a_out, batch, seq)


def _odd_kernel(x_ref, g_ref, w_ref, lng_ref, lnb_ref, ws_ref, bs_ref, mk_ref, mv_ref, wout_ref,
                fg_ref, out_ref, h_ref, p_ref, vn_ref, y_ref, *, final_norm):
    tile_rows = x_ref.shape[0]
    _rmsnorm_rows(x_ref, g_ref, h_ref)
    _project(h_ref, w_ref, p_ref)
    z0 = 2 * C_WIDTH + M_WIDTH

    for r in range(0, tile_rows, ROW_BAND):
        v = p_ref[r:r + ROW_BAND, C_WIDTH:2 * C_WIDTH]
        vc = v - jnp.mean(v, axis=-1, keepdims=True)
        var = jnp.mean(vc * vc, axis=-1, keepdims=True)
        vn_ref[r:r + ROW_BAND, :] = ((vc * lax.rsqrt(var + EPS)) * lng_ref[...] + lnb_ref[...]).astype(BF16)

    res_ref, res0 = (p_ref, C_WIDTH) if final_norm else (out_ref, 0)

    def out_project(k0):
        yk = y_ref[:, k0:k0 + COL_CHUNK]
        for c in range(0, D_MODEL, COL_CHUNK):
            dst = slice(res0 + c, res0 + c + COL_CHUNK)
            base = x_ref[:, c:c + COL_CHUNK] if k0 == 0 else res_ref[:, dst]
            res_ref[:, dst] = base + _dot(yk, wout_ref[k0:k0 + COL_CHUNK, c:c + COL_CHUNK])

    tri_r = lax.broadcasted_iota(jnp.int32, (CHUNK, CHUNK), 0)
    tri_c = lax.broadcasted_iota(jnp.int32, (CHUNK, CHUNK), 1)
    causal = tri_c <= tri_r
    for gi in range(C_GROUPS):
        lanes = slice(gi * C_CH, (gi + 1) * C_CH)
        ws = jnp.where(causal, ws_ref[gi], jnp.zeros((CHUNK, CHUNK), BF16))
        bias = bs_ref[:, gi:gi + 1]
        n_chunks = tile_rows // CHUNK
        vn = jnp.concatenate([vn_ref[n * CHUNK:(n + 1) * CHUNK, lanes] for n in range(n_chunks)], axis=1)
        mixed_all = _dot(ws, vn)
        for n in range(n_chunks):
            rows = slice(n * CHUNK, (n + 1) * CHUNK)
            mixed = mixed_all[:, n * C_CH:(n + 1) * C_CH] + bias
            gate = _silu(p_ref[rows, z0 + gi * C_CH:z0 + (gi + 1) * C_CH])
            y_ref[rows, lanes] = (p_ref[rows, lanes] * mixed * gate).astype(BF16)
        if (gi + 1) * C_CH % COL_CHUNK == 0:
            out_project((gi + 1) * C_CH - COL_CHUNK)

    m_out = _memory_attention(p_ref, 2 * C_WIDTH, mk_ref.at[0, 0], mv_ref.at[0, 0])
    for hd in range(MEM_HEADS):
        c0 = z0 + C_WIDTH + hd * HEAD_DIM
        y0 = C_WIDTH + hd * HEAD_DIM
        y_ref[:, y0:y0 + HEAD_DIM] = (m_out[hd] * _silu(p_ref[:, c0:c0 + HEAD_DIM])).astype(BF16)
    out_project(C_WIDTH)

    if final_norm:
        for r in range(0, tile_rows, ROW_BAND):
            res = p_ref[r:r + ROW_BAND, res0:res0 + D_MODEL]
            ms = jnp.mean(res * res, axis=-1, keepdims=True)
            out_ref[r:r + ROW_BAND, :] = (res * lax.rsqrt(ms + EPS)) * fg_ref[...]


def _odd_layer(x2, memkv, layer, g, w_in, ln_g, ln_b, w_s, b_s, w_out, final_g, final_norm, seq):
    rows = x2.shape[0]
    tile_rows = ODD_TILE
    tiles_per_seq = seq // tile_rows
    row_spec = pl.BlockSpec((tile_rows, D_MODEL), lambda i: (i, 0))
    mk_spec = pl.BlockSpec((1, 1, N_MEM, M_WIDTH), lambda i: (layer, i // tiles_per_seq, 0, 0))
    mv_spec = pl.BlockSpec((1, 1, N_MEM, M_WIDTH), lambda i: (layer, i // tiles_per_seq, 0, 1))
    return pl.pallas_call(
        functools.partial(_odd_kernel, final_norm=final_norm),
        grid=(rows // tile_rows,),
        in_specs=[row_spec, _const_spec((1, D_MODEL)), _layer_spec((D_MODEL, ODD_IN), layer),
                  _const_spec((1, C_WIDTH)), _const_spec((1, C_WIDTH)),
                  _layer_spec((C_GROUPS, CHUNK, CHUNK), layer), _const_spec((CHUNK, C_GROUPS)),
                  mk_spec, mv_spec, _layer_spec((ODD_MIX, D_MODEL), layer), _const_spec((1, D_MODEL))],
        out_specs=row_spec,
        out_shape=jax.ShapeDtypeStruct((rows, D_MODEL), F32),
        scratch_shapes=[pltpu.VMEM((tile_rows, D_MODEL), BF16), pltpu.VMEM((tile_rows, ODD_IN), F32),
                        pltpu.VMEM((tile_rows, C_WIDTH), BF16), pltpu.VMEM((tile_rows, ODD_MIX), BF16)],
        compiler_params=_params("parallel"),
        name="odd_layer",
    )(x2, g.reshape(1, D_MODEL), w_in, ln_g.reshape(1, C_WIDTH), ln_b.reshape(1, C_WIDTH),
      w_s, b_s.T, memkv, memkv, w_out, final_g.reshape(1, D_MODEL))


def kernel(x, mem, positions, g_mem, even_norm_g, even_w_in, even_w_pool, even_pool_scale,
           even_w_mem_kv, even_w_out, odd_norm_g, odd_w_in, odd_ln_g, odd_ln_b, odd_w_s,
           odd_b_s, odd_w_mem_kv, odd_w_out, final_norm_g):
    batch, seq, _ = x.shape
    depth = even_norm_g.shape[0] + odd_norm_g.shape[0]
    assert all(seq % t == 0 for t in (QKV_TILE, TAIL_TILE, ODD_TILE, SUPER))

    tabs = _rope_tables(positions)
    memkv_even = _memory_kv(mem, g_mem, even_w_mem_kv)
    memkv_odd = _memory_kv(mem, g_mem, odd_w_mem_kv)
    even_w_qkv, even_w_rest = _even_weights(even_w_in)
    even_w_pool, even_w_out, odd_w_in, odd_w_s, odd_w_out = (
        w.astype(BF16) for w in (even_w_pool, even_w_out, odd_w_in, odd_w_s, odd_w_out))

    x2 = x.reshape(batch * seq, D_MODEL)
    for layer in range(depth):
        i = layer // 2
        if layer % 2 == 0:
            x2 = _even_layer(x2, tabs, memkv_even, i, even_norm_g[i], even_w_qkv, even_w_rest, even_w_pool,
                             even_pool_scale[i], even_w_out, batch, seq)
        else:
            x2 = _odd_layer(x2, memkv_odd, i, odd_norm_g[i], odd_w_in, odd_ln_g[i], odd_ln_b[i],
                            odd_w_s, odd_b_s[i], odd_w_out, final_norm_g,
                            layer == depth - 1, seq)
    return x2.reshape(batch, seq, D_MODEL)
```
